```python
import math
import jax, jax.numpy as jnp
from jax import lax
import numpy as np

D_MODEL = 1024
BATCH = 2
SEQ = 16384
DEPTH = 2

SSM_GROUPS = 32
SSM_GROUP_CH = 16
SSM_WIDTH = SSM_GROUPS * SSM_GROUP_CH
SSM_STATE = 64
DT_MIN = 1e-3
DT_MAX = 1e-1
ATTN_HEADS = 4
ATTN_HEAD_DIM = 64
ATTN_V_DIM = 2 * ATTN_HEAD_DIM
QK_WIDTH = ATTN_HEADS * 2 * ATTN_HEAD_DIM
ATTN_WIDTH = ATTN_HEADS * ATTN_V_DIM
Q_BLOCK = 128
NEG_BIG = -1e30
IN_SIZES = (SSM_WIDTH, QK_WIDTH, QK_WIDTH, ATTN_WIDTH, D_MODEL, D_MODEL)
IN_WIDTH = sum(IN_SIZES)
IN_SPLITS = [int(s) for s in np.cumsum(IN_SIZES)[:-1]]
PEER_HEADS = 8
PEER_N_KEYS = 128
PEER_N_EXPERTS = PEER_N_KEYS * PEER_N_KEYS
PEER_KEY_DIM = 256
PEER_HALF = PEER_KEY_DIM // 2
PEER_TOPK = 16
PEER_TOKEN_BLOCK = 128
RMS_EPS = 1e-6

kernel_name = 'hybrid_s5_diffattn_peer_adaln'


def rms_norm(x, w):
    xf = x.astype(jnp.float32)
    y = xf * lax.rsqrt(jnp.mean(xf * xf, axis=-1, keepdims=True) + RMS_EPS)
    return (y * w.astype(jnp.float32)).astype(x.dtype)


def _complex_affine_combine(left, right):
    a1r, a1i, b1r, b1i = left
    a2r, a2i, b2r, b2i = right
    ar = a2r * a1r - a2i * a1i
    ai = a2r * a1i + a2i * a1r
    br = a2r * b1r - a2i * b1i + b2r
    bi = a2r * b1i + a2i * b1r + b2i
    return (ar, ai, br, bi)


def s5_branch(u, log_dt, a_re, a_im, b_re, b_im, c_re, c_im, d_skip, glu_w):
    f32 = jnp.float32
    bsz, s_len, _ = u.shape
    uf = u.astype(f32).reshape(bsz, s_len, SSM_GROUPS, SSM_GROUP_CH)
    dt = jnp.exp(log_dt.astype(f32))[:, None]
    lr = a_re.astype(f32)
    li = a_im.astype(f32)
    mag = jnp.exp(lr * dt)
    ab_re = mag * jnp.cos(li * dt)
    ab_im = mag * jnp.sin(li * dt)
    den = lr * lr + li * li
    nr = ab_re - 1.0
    f_re = (nr * lr + ab_im * li) / den
    f_im = (ab_im * lr - nr * li) / den
    br = b_re.astype(f32)
    bi = b_im.astype(f32)
    bb_re = f_re[..., None] * br - f_im[..., None] * bi
    bb_im = f_re[..., None] * bi + f_im[..., None] * br
    bu_re = jnp.einsum('bsgh,gph->sbgp', uf, bb_re)
    bu_im = jnp.einsum('bsgh,gph->sbgp', uf, bb_im)
    shp = (s_len, 1, SSM_GROUPS, SSM_STATE)
    a_r = jnp.broadcast_to(ab_re[None, None], shp)
    a_i = jnp.broadcast_to(ab_im[None, None], shp)
    _, _, xr, xi = lax.associative_scan(_complex_affine_combine, (a_r, a_i, bu_re, bu_im), axis=0)
    y = (jnp.einsum('sbgp,ghp->bsgh', xr, c_re.astype(f32))
         - jnp.einsum('sbgp,ghp->bsgh', xi, c_im.astype(f32))
         + d_skip.astype(f32) * uf)
    y = jax.nn.gelu(y.reshape(bsz, s_len, SSM_WIDTH))
    val, gate = jnp.split(y @ glu_w.astype(f32), 2, axis=-1)
    return (val * jax.nn.sigmoid(gate)).astype(u.dtype)


def diff_attention(q, k, v, q_norm_w, k_norm_w, lq1, lk1, lq2, lk2, subln_w, lambda_init):
    f32 = jnp.float32
    bsz, s_len, _ = q.shape
    q = rms_norm(q.reshape(bsz, s_len, ATTN_HEADS, 2, ATTN_HEAD_DIM), q_norm_w)
    k = rms_norm(k.reshape(bsz, s_len, ATTN_HEADS, 2, ATTN_HEAD_DIM), k_norm_w)
    vf = v.reshape(bsz, s_len, ATTN_HEADS, ATTN_V_DIM).astype(f32)
    kf = k.astype(f32)
    lam = (jnp.exp(jnp.sum(lq1.astype(f32) * lk1.astype(f32)))
           - jnp.exp(jnp.sum(lq2.astype(f32) * lk2.astype(f32))) + lambda_init)
    scale = ATTN_HEAD_DIM ** -0.5
    n_blk = s_len // Q_BLOCK
    qb = q.astype(f32).reshape(bsz, n_blk, Q_BLOCK, ATTN_HEADS, 2, ATTN_HEAD_DIM).transpose(1, 0, 2, 3, 4, 5)
    key_pos = jnp.arange(s_len)

    def block(args):
        q_blk, blk_idx = args
        s = jnp.einsum('bqhcd,bkhcd->bhcqk', q_blk, kf) * scale
        q_pos = blk_idx * Q_BLOCK + jnp.arange(Q_BLOCK)
        causal = key_pos[None, :] <= q_pos[:, None]
        s = jnp.where(causal, s, NEG_BIG)
        p = jax.nn.softmax(s, axis=-1)
        w = p[:, :, 0] - lam * p[:, :, 1]
        return jnp.einsum('bhqk,bkhd->bqhd', w, vf)

    o = lax.map(block, (qb, jnp.arange(n_blk)))
    o = o.transpose(1, 0, 2, 3, 4).reshape(bsz, s_len, ATTN_HEADS, ATTN_V_DIM)
    o = rms_norm(o, subln_w) * (1.0 - lambda_init)
    return o.reshape(bsz, s_len, ATTN_WIDTH).astype(v.dtype)


def peer_ffn(h, w_q, sub_k1, sub_k2, u_tab, v_tab):
    f32 = jnp.float32
    bsz, s_len, d = h.shape
    n_tok = bsz * s_len
    n_blk = n_tok // PEER_TOKEN_BLOCK
    hb = h.reshape(n_blk, PEER_TOKEN_BLOCK, d)

    def block(xb):
        q = (xb @ w_q).reshape(PEER_TOKEN_BLOCK, PEER_HEADS, 2, PEER_HALF)
        s1 = jnp.einsum('thd,nd->thn', q[:, :, 0], sub_k1).astype(f32)
        s2 = jnp.einsum('thd,nd->thn', q[:, :, 1], sub_k2).astype(f32)
        v1, i1 = lax.top_k(s1, PEER_TOPK)
        v2, i2 = lax.top_k(s2, PEER_TOPK)
        cand = (v1[..., :, None] + v2[..., None, :]).reshape(PEER_TOKEN_BLOCK, PEER_HEADS, PEER_TOPK * PEER_TOPK)
        sc, ci = lax.top_k(cand, PEER_TOPK)
        e1 = jnp.take_along_axis(i1, ci // PEER_TOPK, axis=-1)
        e2 = jnp.take_along_axis(i2, ci % PEER_TOPK, axis=-1)
        expert = e1 * PEER_N_KEYS + e2
        g = jax.nn.softmax(sc, axis=-1)
        u_sel = u_tab[expert]
        act = jax.nn.gelu(jnp.einsum('thkd,td->thk', u_sel, xb).astype(f32))
        v_sel = v_tab[expert]
        return jnp.einsum('thk,thkd->td', (g * act).astype(v_sel.dtype), v_sel)

    out = lax.map(block, hb)
    return out.reshape(bsz, s_len, d).astype(h.dtype)


def setup_inputs(seed: int = 0) -> dict:
    key = jax.random.key(seed)
    ks = jax.random.split(key, 32)
    L, D = DEPTH, D_MODEL
    G, H, P = SSM_GROUPS, SSM_GROUP_CH, SSM_STATE
    nrm = lambda k, shp, s: jax.random.normal(k, shp, jnp.float32) * s
    return {
        'x': nrm(ks[0], (BATCH, SEQ, D), 1.0),
        'c': nrm(ks[1], (BATCH, D), 1.0),
        'ada_w': nrm(ks[2], (L, D, 6 * D), 0.5 * D ** -0.5),
        'ada_b': nrm(ks[3], (L, 6 * D), 0.02),
        'norm1_w': 1.0 + nrm(ks[4], (L, D), 0.02),
        'norm2_w': 1.0 + nrm(ks[5], (L, D), 0.02),
        'w_in': nrm(ks[6], (L, D, IN_WIDTH), D ** -0.5),
        'ssm_log_dt': jax.random.uniform(ks[7], (L, G), jnp.float32, math.log(DT_MIN), math.log(DT_MAX)),
        'ssm_a_re': -0.5 + nrm(ks[8], (L, G, P), 0.01),
        'ssm_a_im': jnp.pi * jnp.arange(P, dtype=jnp.float32)[None, None, :] + nrm(ks[9], (L, G, P), 0.01),
        'ssm_b_re': nrm(ks[10], (L, G, P, H), (2 * H) ** -0.5),
        'ssm_b_im': nrm(ks[11], (L, G, P, H), (2 * H) ** -0.5),
        'ssm_c_re': nrm(ks[12], (L, G, H, P), (2 * P) ** -0.5),
        'ssm_c_im': nrm(ks[13], (L, G, H, P), (2 * P) ** -0.5),
        'ssm_d': nrm(ks[14], (L, G, H), 1.0),
        'ssm_glu_w': nrm(ks[15], (L, SSM_WIDTH, 2 * D), SSM_WIDTH ** -0.5),
        'q_norm_w': 1.0 + nrm(ks[16], (L, ATTN_HEAD_DIM), 0.02),
        'k_norm_w': 1.0 + nrm(ks[17], (L, ATTN_HEAD_DIM), 0.02),
        'lambda_q1': nrm(ks[18], (L, ATTN_HEAD_DIM), 0.1),
        'lambda_k1': nrm(ks[19], (L, ATTN_HEAD_DIM), 0.1),
        'lambda_q2': nrm(ks[20], (L, ATTN_HEAD_DIM), 0.1),
        'lambda_k2': nrm(ks[21], (L, ATTN_HEAD_DIM), 0.1),
        'subln_w': 1.0 + nrm(ks[22], (L, ATTN_V_DIM), 0.02),
        'attn_up_w': nrm(ks[23], (L, ATTN_WIDTH, D), ATTN_WIDTH ** -0.5),
        'w_out': nrm(ks[24], (L, D, D), D ** -0.5),
        'peer_wq': nrm(ks[25], (L, D, PEER_HEADS * PEER_KEY_DIM), D ** -0.5),
        'peer_k1': nrm(ks[26], (L, PEER_N_KEYS, PEER_HALF), PEER_HALF ** -0.5),
        'peer_k2': nrm(ks[27], (L, PEER_N_KEYS, PEER_HALF), PEER_HALF ** -0.5),
        'peer_u': nrm(ks[28], (L, PEER_N_EXPERTS, D), D ** -0.5),
        'peer_v': nrm(ks[29], (L, PEER_N_EXPERTS, D), PEER_HEADS ** -0.5),
    }


def reference(x, c, ada_w, ada_b, norm1_w, norm2_w, w_in, ssm_log_dt, ssm_a_re, ssm_a_im,
              ssm_b_re, ssm_b_im, ssm_c_re, ssm_c_im, ssm_d, ssm_glu_w, q_norm_w, k_norm_w,
              lambda_q1, lambda_k1, lambda_q2, lambda_k2, subln_w, attn_up_w, w_out,
              peer_wq, peer_k1, peer_k2, peer_u, peer_v):
    c_act = jax.nn.silu(c)
    for l in range(DEPTH):
        mod = c_act @ ada_w[l] + ada_b[l]
        sh1, sc1, g1, sh2, sc2, g2 = jnp.split(mod, 6, axis=-1)
        h = rms_norm(x, norm1_w[l]) * (1.0 + sc1[:, None, :]) + sh1[:, None, :]
        z = h @ w_in[l]
        u_ssm, q, k, v, gate_ssm, gate_attn = jnp.split(z, IN_SPLITS, axis=-1)
        y_ssm = s5_branch(u_ssm, ssm_log_dt[l], ssm_a_re[l], ssm_a_im[l], ssm_b_re[l], ssm_b_im[l],
                          ssm_c_re[l], ssm_c_im[l], ssm_d[l], ssm_glu_w[l])
        lambda_init = 0.8 - 0.6 * math.exp(-0.3 * l)
        y_att = diff_attention(q, k, v, q_norm_w[l], k_norm_w[l], lambda_q1[l], lambda_k1[l],
                               lambda_q2[l], lambda_k2[l], subln_w[l], lambda_init)
        y_att = y_att @ attn_up_w[l]
        merged = jax.nn.sigmoid(gate_ssm) * y_ssm + jax.nn.sigmoid(gate_attn) * y_att
        x = x + g1[:, None, :] * (merged @ w_out[l])
        h2 = rms_norm(x, norm2_w[l]) * (1.0 + sc2[:, None, :]) + sh2[:, None, :]
        x = x + g2[:, None, :] * peer_ffn(h2, peer_wq[l], peer_k1[l], peer_k2[l], peer_u[l], peer_v[l])
    return x
```

```python
import functools
import math

import jax
import jax.numpy as jnp
import numpy as np
from jax import lax
from jax.experimental import pallas as pl
from jax.experimental.pallas import tpu as pltpu

F32 = jnp.float32
BF16 = jnp.bfloat16

D_MODEL = 1024
SSM_GROUPS = 32
SSM_GROUP_CH = 16
SSM_WIDTH = SSM_GROUPS * SSM_GROUP_CH
SSM_STATE = 64
ATTN_HEADS = 4
ATTN_HEAD_DIM = 64
ATTN_V_DIM = 2 * ATTN_HEAD_DIM
QK_WIDTH = ATTN_HEADS * 2 * ATTN_HEAD_DIM
ATTN_WIDTH = ATTN_HEADS * ATTN_V_DIM
PEER_HEADS = 8
PEER_N_KEYS = 128
PEER_HALF = 128
PEER_TOPK = 16
PEER_SLOTS = PEER_HEADS * PEER_TOPK
PEER_HALF_EXPERTS = PEER_N_KEYS * PEER_N_KEYS // 2
RMS_EPS = 1e-6
NEG_BIG = -1e30
LOG2E = 1.4426950408889634

LANES = 128
SUBLANES = 8
S5_CHUNK = LANES
VMEM_LIMIT = 56 * 1024 * 1024


def _cparams(*sem):
    return pltpu.CompilerParams(dimension_semantics=sem, vmem_limit_bytes=VMEM_LIMIT)


def _ada_kernel(c_ref, w_ref, b_ref, o_ref):
    c = c_ref[...]
    ca = c * jax.nn.sigmoid(c)
    o_ref[0] = jnp.dot(ca, w_ref[0], preferred_element_type=F32) + b_ref[0]


def ada_modulation(c, ada_w, ada_b):
    depth, d, d6 = ada_w.shape
    bsz = c.shape[0]
    rows = -(-bsz // SUBLANES) * SUBLANES
    c_pad = jnp.zeros((rows, d), F32).at[:bsz].set(c)
    out = pl.pallas_call(
        _ada_kernel,
        grid=(depth, d6 // d),
        in_specs=[
            pl.BlockSpec((rows, d), lambda l, j: (0, 0)),
            pl.BlockSpec((1, d, d), lambda l, j: (l, 0, j)),
            pl.BlockSpec((1, 1, d), lambda l, j: (l, 0, j)),
        ],
        out_specs=pl.BlockSpec((1, rows, d), lambda l, j: (l, 0, j)),
        out_shape=jax.ShapeDtypeStruct((depth, rows, d6), F32),
        compiler_params=_cparams("arbitrary", "arbitrary"),
        name="ada_modulation",
    )(c_pad, ada_w, ada_b.reshape(depth, 1, d6))
    return out[:, :bsz]


def _group_norm64(z, gmat, w_row):
    ssq = jnp.dot((z * z).astype(BF16), gmat, preferred_element_type=F32)
    return z * lax.rsqrt(ssq * (1.0 / ATTN_HEAD_DIM) + RMS_EPS) * w_row


def _inproj_kernel(x_ref, nw_ref, sc_ref, sh_ref, w_ref, qw_ref, kw_ref, gmat_ref,
                   u_ref, q_ref, k_ref, v_ref, gs_ref, ga_ref):
    x = x_ref[...]
    ms = jnp.mean(x * x, axis=-1, keepdims=True)
    h = x * lax.rsqrt(ms + RMS_EPS) * nw_ref[...]
    h = h * (1.0 + sc_ref[0]) + sh_ref[0]
    hb = h.astype(BF16)
    o = 0
    u_ref[...] = jnp.dot(hb, w_ref[:, o:o + SSM_WIDTH], preferred_element_type=F32).astype(BF16)
    o += SSM_WIDTH
    q = jnp.dot(hb, w_ref[:, o:o + QK_WIDTH], preferred_element_type=F32)
    q_ref[...] = _group_norm64(q, gmat_ref[...], qw_ref[...]).astype(BF16)
    o += QK_WIDTH
    k = jnp.dot(hb, w_ref[:, o:o + QK_WIDTH], preferred_element_type=F32)
    k_ref[...] = _group_norm64(k, gmat_ref[...], kw_ref[...]).astype(BF16)
    o += QK_WIDTH
    v_ref[...] = jnp.dot(hb, w_ref[:, o:o + ATTN_WIDTH], preferred_element_type=F32).astype(BF16)
    o += ATTN_WIDTH
    gs = jnp.dot(hb, w_ref[:, o:o + D_MODEL], preferred_element_type=F32)
    gs_ref[...] = jax.nn.sigmoid(gs).astype(BF16)
    o += D_MODEL
    ga = jnp.dot(hb, w_ref[:, o:o + D_MODEL], preferred_element_type=F32)
    ga_ref[...] = jax.nn.sigmoid(ga).astype(BF16)


def in_projection(x2d, seq, norm_w, sc, sh, w_in, q_norm_w, k_norm_w, tm=512):
    n, d = x2d.shape
    tm = min(tm, seq)
    assert seq % tm == 0 and n % tm == 0
    bsz = n // seq
    width = w_in.shape[1]
    qw = jnp.tile(q_norm_w, QK_WIDTH // ATTN_HEAD_DIM)[None] * (ATTN_HEAD_DIM ** -0.5 * LOG2E)
    kw = jnp.tile(k_norm_w, QK_WIDTH // ATTN_HEAD_DIM)[None]
    grp = np.arange(QK_WIDTH) // ATTN_HEAD_DIM
    gmat = jnp.asarray(grp[:, None] == grp[None, :], BF16)
    row = lambda i: (i, 0)
    fixed = lambda i: (0, 0)
    per_b = lambda i: (i * tm // seq, 0, 0)
    out_w = (SSM_WIDTH, QK_WIDTH, QK_WIDTH, ATTN_WIDTH, D_MODEL, D_MODEL)
    return pl.pallas_call(
        _inproj_kernel,
        grid=(n // tm,),
        in_specs=[
            pl.BlockSpec((tm, d), row),
            pl.BlockSpec((1, d), fixed),
            pl.BlockSpec((1, 1, d), per_b),
            pl.BlockSpec((1, 1, d), per_b),
            pl.BlockSpec((d, width), fixed),
            pl.BlockSpec((1, QK_WIDTH), fixed),
            pl.BlockSpec((1, QK_WIDTH), fixed),
            pl.BlockSpec((QK_WIDTH, QK_WIDTH), fixed),
        ],
        out_specs=[pl.BlockSpec((tm, w), row) for w in out_w],
        out_shape=[jax.ShapeDtypeStruct((n, w), BF16) for w in out_w],
        compiler_params=_cparams("arbitrary"),
        name="in_projection",
    )(x2d, norm_w[None], sc.reshape(bsz, 1, d), sh.reshape(bsz, 1, d), w_in.astype(BF16), qw, kw, gmat)


def _attn_kernel(lam_ref, q_ref, k_ref, v_ref, sw_ref, o_ref, m_scr, acc_scr, *, blk, out_scale):
    qi = pl.program_id(2)
    q = q_ref[...]
    lane = lax.broadcasted_iota(jnp.int32, q.shape, 1)
    zero = jnp.zeros_like(q)
    qq = jnp.concatenate([jnp.where(lane < ATTN_HEAD_DIM, q, zero),
                          jnp.where(lane >= ATTN_HEAD_DIM, q, zero)], axis=0)
    ones = jnp.ones((blk, LANES), BF16)

    m_scr[...] = jnp.full(m_scr.shape, NEG_BIG, F32)
    acc_scr[...] = jnp.zeros(acc_scr.shape, F32)

    def step(ki, masked):
        m = m_scr[...]
        off = pl.multiple_of(ki * blk, blk)
        kb = k_ref[pl.ds(off, blk), :]
        vb = jnp.concatenate([v_ref[pl.ds(off, blk), :], ones], axis=1)
        s = lax.dot_general(qq, kb, (((1,), (1,)), ((), ())), preferred_element_type=F32)
        if masked:
            row = lax.broadcasted_iota(jnp.int32, s.shape, 0)
            col = lax.broadcasted_iota(jnp.int32, s.shape, 1)
            row = jnp.where(row >= blk, row - blk, row)
            s = jnp.where(col <= row, s, NEG_BIG)
        m_new = jnp.maximum(m, jnp.max(s, axis=1, keepdims=True))
        alpha = jnp.exp2(m - m_new)
        p = jnp.exp2(s - m_new).astype(BF16)
        acc_scr[...] = alpha * acc_scr[...] + jnp.dot(p, vb, preferred_element_type=F32)
        m_scr[...] = m_new

    @pl.loop(0, qi)
    def _(ki):
        step(ki, False)

    step(qi, True)
    acc = acc_scr[...]
    o_all = acc[:, :LANES] / acc[:, LANES:]
    o = o_all[:blk] - lam_ref[0] * o_all[blk:]
    ms = jnp.mean(o * o, axis=-1, keepdims=True)
    o_ref[...] = (o * lax.rsqrt(ms + RMS_EPS) * sw_ref[...] * out_scale).astype(BF16)


def diff_attention(q, k, v, seq, lam, subln_w, lambda_init, blk=512):
    n = q.shape[0]
    blk = min(blk, seq)
    assert seq % blk == 0
    bsz = n // seq
    nq = seq // blk
    return pl.pallas_call(
        functools.partial(_attn_kernel, blk=blk, out_scale=1.0 - lambda_init),
        grid=(bsz, ATTN_HEADS, nq),
        in_specs=[
            pl.BlockSpec(memory_space=pltpu.SMEM),
            pl.BlockSpec((blk, ATTN_V_DIM), lambda b, h, i: (b * nq + i, h)),
            pl.BlockSpec((seq, ATTN_V_DIM), lambda b, h, i: (b, h)),
            pl.BlockSpec((seq, ATTN_V_DIM), lambda b, h, i: (b, h)),
            pl.BlockSpec((1, ATTN_V_DIM), lambda b, h, i: (0, 0)),
        ],
        out_specs=pl.BlockSpec((blk, ATTN_V_DIM), lambda b, h, i: (b * nq + i, h)),
        out_shape=jax.ShapeDtypeStruct((n, ATTN_WIDTH), BF16),
        scratch_shapes=[pltpu.VMEM((2 * blk, 1), F32), pltpu.VMEM((2 * blk, 2 * LANES), F32)],
        compiler_params=_cparams("arbitrary", "arbitrary", "arbitrary"),
        name="diff_attention",
    )(lam.reshape(1), q, k, v, subln_w[None])


def s5_matrices(log_dt, a_re, a_im, b_re, b_im, c_re, c_im, d_skip):
    t_len = S5_CHUNK
    g, p, h = b_re.shape
    dt = jnp.exp(log_dt)[:, None]
    mag = jnp.exp(a_re * dt)
    ab_re = mag * jnp.cos(a_im * dt)
    ab_im = mag * jnp.sin(a_im * dt)
    den = a_re * a_re + a_im * a_im
    nr = ab_re - 1.0
    f_re = (nr * a_re + ab_im * a_im) / den
    f_im = (ab_im * a_re - nr * a_im) / den
    bb_re = f_re[..., None] * b_re - f_im[..., None] * b_im
    bb_im = f_re[..., None] * b_im + f_im[..., None] * b_re
    tau = jnp.arange(t_len + 1, dtype=F32)[None, :, None]
    pmag = jnp.exp(a_re[:, None, :] * dt[:, None, :] * tau)
    pw_re = pmag * jnp.cos(a_im[:, None, :] * dt[:, None, :] * tau)
    pw_im = pmag * jnp.sin(a_im[:, None, :] * dt[:, None, :] * tau)
    ca_re = c_re[:, None] * pw_re[:, :, None, :] - c_im[:, None] * pw_im[:, :, None, :]
    ca_im = c_re[:, None] * pw_im[:, :, None, :] + c_im[:, None] * pw_re[:, :, None, :]
    kern = (jnp.einsum("gtop,gpi->gtoi", ca_re[:, :t_len], bb_re)
            - jnp.einsum("gtop,gpi->gtoi", ca_im[:, :t_len], bb_im))
    kern = kern.at[:, 0].add(jax.vmap(jnp.diag)(d_skip))
    s_idx = jnp.arange(t_len)[:, None]
    t_idx = jnp.arange(t_len)[None, :]
    lag = t_idx - s_idx
    toep = jnp.where((lag >= 0)[None, :, :, None, None], kern[:, jnp.maximum(lag, 0)], 0.0)
    toep = toep.transpose(0, 4, 1, 3, 2).reshape(g, h * t_len, h * t_len)
    rp_re = pw_re[:, t_len - 1::-1][:, :t_len]
    rp_im = pw_im[:, t_len - 1::-1][:, :t_len]
    w_re = rp_re[..., None] * bb_re[:, None] - rp_im[..., None] * bb_im[:, None]
    w_im = rp_re[..., None] * bb_im[:, None] + rp_im[..., None] * bb_re[:, None]
    w_st = jnp.concatenate([w_re, w_im], axis=2).transpose(0, 3, 1, 2).reshape(g, h * t_len, 2 * p)
    v_st = jnp.concatenate([ca_re[:, 1:], -ca_im[:, 1:]], axis=3)
    v_st = v_st.transpose(0, 3, 2, 1).reshape(g, 2 * p, h * t_len)
    at_re, at_im = pw_re[:, t_len], pw_im[:, t_len]
    coef = jnp.stack([jnp.concatenate([at_re, at_re], -1), jnp.concatenate([-at_im, at_im], -1),
                      jnp.concatenate([at_im, -at_im], -1)], axis=1)
    return toep.astype(BF16), w_st.astype(BF16), v_st.astype(BF16), coef


def _s5_kernel(u_ref, toep_ref, w_ref, v_ref, coef_ref, y_ref, s_scr, sw_scr, xp_scr, *, bsz, nchunk):
    hch = u_ref.shape[1]
    xin = jnp.concatenate([u_ref[0, i] for i in range(hch)], axis=1)
    y = jnp.dot(xin, toep_ref[0], preferred_element_type=F32)
    s = jnp.dot(xin, w_ref[0], preferred_element_type=F32)
    half = s.shape[1] // 2
    s_scr[...] = s
    sw_scr[...] = pltpu.roll(s, half, 1)
    coef = coef_ref[0]
    a_cat, b_cat, b_swp = coef[0:1], coef[1:2], coef[2:3]

    def body(c8, carry):
        new = []
        for b in range(bsz):
            x, xs = carry[2 * b], carry[2 * b + 1]
            base = pl.multiple_of(b * nchunk + c8 * SUBLANES, SUBLANES)
            st = s_scr[pl.ds(base, SUBLANES), :]
            sw = sw_scr[pl.ds(base, SUBLANES), :]
            prev = []
            for i in range(SUBLANES):
                prev.append(x)
                x, xs = (a_cat * x + b_cat * xs + st[i:i + 1], a_cat * xs + b_swp * x + sw[i:i + 1])
            xp_scr[pl.ds(base, SUBLANES), :] = jnp.concatenate(prev, axis=0)
            new += [x, xs]
        return tuple(new)

    zero = jnp.zeros((1, s.shape[1]), F32)
    lax.fori_loop(0, nchunk // SUBLANES, body, (zero,) * (2 * bsz))
    xp = xp_scr[...]
    xp_hi = xp.astype(BF16)
    xp_lo = (xp - xp_hi.astype(F32)).astype(BF16)
    y = y + jnp.dot(xp_hi, v_ref[0], preferred_element_type=F32) + jnp.dot(xp_lo, v_ref[0], preferred_element_type=F32)
    y = jax.nn.gelu(y)
    t_len = y.shape[1] // hch
    for i in range(hch):
        y_ref[0, i] = y[:, i * t_len:(i + 1) * t_len].astype(BF16)


def s5_mix(u_t, bsz, seq, toep, w_st, v_st, coef):
    t_len = S5_CHUNK
    n = bsz * seq
    assert seq % (t_len * SUBLANES) == 0
    nchunk = seq // t_len
    rows = n // t_len
    g, hch = SSM_GROUPS, SSM_GROUP_CH
    u4 = u_t.reshape(g, hch, rows, t_len)
    kt = hch * t_len
    p2 = 2 * SSM_STATE
    grp = lambda i: (i, 0, 0)
    y4 = pl.pallas_call(
        functools.partial(_s5_kernel, bsz=bsz, nchunk=nchunk),
        grid=(g,),
        in_specs=[
            pl.BlockSpec((1, hch, rows, t_len), lambda i: (i, 0, 0, 0)),
            pl.BlockSpec((1, kt, kt), grp),
            pl.BlockSpec((1, kt, p2), grp),
            pl.BlockSpec((1, p2, kt), grp),
            pl.BlockSpec((1, 3, p2), grp),
        ],
        out_specs=pl.BlockSpec((1, hch, rows, t_len), lambda i: (i, 0, 0, 0)),
        out_shape=jax.ShapeDtypeStruct((g, hch, rows, t_len), BF16),
        scratch_shapes=[pltpu.VMEM((rows, p2), F32)] * 3,
        compiler_params=_cparams("arbitrary"),
        name="s5_mix",
    )(u4, toep, w_st, v_st, coef)
    return y4.reshape(g * hch, n)


def _mid_kernel(x_ref, yg_ref, oa_ref, gs_ref, ga_ref, g1_ref, sc2_ref, sh2_ref, nw2_ref,
                glu_ref, up_ref, wo_ref, wq_ref, k1_ref, k2_ref,
                x1_ref, h2_ref, s1_ref, s2_ref):
    vg = jnp.dot(yg_ref[...], glu_ref[...], preferred_element_type=F32)
    y_ssm = vg[:, :D_MODEL] * jax.nn.sigmoid(vg[:, D_MODEL:])
    y_att = jnp.dot(oa_ref[...], up_ref[...], preferred_element_type=F32)
    merged = gs_ref[...].astype(F32) * y_ssm + ga_ref[...].astype(F32) * y_att
    x1 = x_ref[...] + g1_ref[0] * jnp.dot(merged.astype(BF16), wo_ref[...], preferred_element_type=F32)
    x1_ref[...] = x1
    ms = jnp.mean(x1 * x1, axis=-1, keepdims=True)
    h2 = x1 * lax.rsqrt(ms + RMS_EPS) * nw2_ref[...]
    h2 = h2 * (1.0 + sc2_ref[0]) + sh2_ref[0]
    h2_ref[...] = h2
    qp = jnp.dot(h2.astype(BF16), wq_ref[...], preferred_element_type=F32).astype(BF16)
    nt = (((1,), (1,)), ((), ()))
    for hh in range(PEER_HEADS):
        o = hh * 2 * PEER_HALF
        s1_ref[hh] = lax.dot_general(k1_ref[...], qp[:, o:o + PEER_HALF], nt, preferred_element_type=F32)
        s2_ref[hh] = lax.dot_general(k2_ref[...], qp[:, o + PEER_HALF:o + 2 * PEER_HALF], nt,
                                     preferred_element_type=F32)


def mid_block(x2d, seq, yg, o_att, gs, ga, g1, sc2, sh2, norm2_w, glu_w, up_w, w_out, peer_wq, k1, k2, tm=256):
    n, d = x2d.shape
    tm = min(tm, seq)
    assert seq % tm == 0
    bsz = n // seq
    row = lambda i: (i, 0)
    fixed = lambda i: (0, 0)
    per_b = lambda i: (i * tm // seq, 0, 0)
    full = lambda a: pl.BlockSpec(a.shape, fixed)
    weights = [w.astype(BF16) for w in (glu_w, up_w, w_out, peer_wq, k1, k2)]
    sc_shape = (PEER_HEADS, PEER_N_KEYS, n)
    return pl.pallas_call(
        _mid_kernel,
        grid=(n // tm,),
        in_specs=[
            pl.BlockSpec((tm, d), row),
            pl.BlockSpec((tm, SSM_WIDTH), row),
            pl.BlockSpec((tm, ATTN_WIDTH), row),
            pl.BlockSpec((tm, d), row),
            pl.BlockSpec((tm, d), row),
            pl.BlockSpec((1, 1, d), per_b),
            pl.BlockSpec((1, 1, d), per_b),
            pl.BlockSpec((1, 1, d), per_b),
            pl.BlockSpec((1, d), fixed),
        ] + [full(w) for w in weights],
        out_specs=[
            pl.BlockSpec((tm, d), row),
            pl.BlockSpec((tm, d), row),
            pl.BlockSpec((PEER_HEADS, PEER_N_KEYS, tm), lambda i: (0, 0, i)),
            pl.BlockSpec((PEER_HEADS, PEER_N_KEYS, tm), lambda i: (0, 0, i)),
        ],
        out_shape=[jax.ShapeDtypeStruct((n, d), F32), jax.ShapeDtypeStruct((n, d), F32),
                   jax.ShapeDtypeStruct(sc_shape, F32), jax.ShapeDtypeStruct(sc_shape, F32)],
        compiler_params=_cparams("arbitrary"),
        name="mid_block",
    )(x2d, yg, o_att, gs, ga, g1.reshape(bsz, 1, d), sc2.reshape(bsz, 1, d), sh2.reshape(bsz, 1, d),
      norm2_w[None], *weights)


def _top16_rows(s):
    nrows = s.shape[0]
    rowid = lax.broadcasted_iota(jnp.int32, s.shape, 0)
    vals, ids = [], []
    for _ in range(PEER_TOPK):
        m = jnp.max(s, axis=0, keepdims=True)
        i = jnp.min(jnp.where(s == m, rowid, nrows), axis=0, keepdims=True)
        vals.append(m)
        ids.append(i)
        s = jnp.where(rowid == i, -jnp.inf, s)
    return jnp.concatenate(vals, axis=0), jnp.concatenate(ids, axis=0)


def _topk_kernel(s1_ref, s2_ref, g_ref, e_ref):
    v1, i1 = _top16_rows(s1_ref[0])
    v2, i2 = _top16_rows(s2_ref[0])
    half = PEER_TOPK // 2
    sub = lax.broadcasted_iota(jnp.int32, (half, v1.shape[1]), 0)
    cand, pay, flat = [], [], []
    for i in range(half):
        cand.append(v1[i:i + 1] + v2[:half])
        pay.append(i1[i:i + 1] * PEER_N_KEYS + i2[:half])
        flat.append(sub + i * PEER_TOPK)
    cand.append(v1[0:1] + v2[half:])
    pay.append(i1[0:1] * PEER_N_KEYS + i2[half:])
    flat.append(sub + half)
    cand.append(v1[half:] + v2[0:1])
    pay.append(i1[half:] * PEER_N_KEYS + i2[0:1])
    flat.append((sub + half) * PEER_TOPK)
    cand = jnp.concatenate(cand, axis=0)
    pay = jnp.concatenate(pay, axis=0)
    flat = jnp.concatenate(flat, axis=0)
    big = PEER_TOPK * PEER_TOPK
    sc, ex = [], []
    for _ in range(PEER_TOPK):
        m = jnp.max(cand, axis=0, keepdims=True)
        f = jnp.min(jnp.where(cand == m, flat, big), axis=0, keepdims=True)
        sel = flat == f
        ex.append(jnp.max(jnp.where(sel, pay, -1), axis=0, keepdims=True))
        sc.append(m)
        cand = jnp.where(sel, -jnp.inf, cand)
    sc = jnp.concatenate(sc, axis=0)
    p = jnp.exp(sc - sc[0:1])
    g_ref[0] = p / jnp.sum(p, axis=0, keepdims=True)
    e_ref[0] = jnp.concatenate(ex, axis=0)


def peer_topk(s1, s2, tn=512):
    heads, nkeys, n = s1.shape
    tn = min(tn, n)
    assert n % tn == 0
    blk_in = pl.BlockSpec((1, nkeys, tn), lambda h, i: (h, 0, i))
    blk_out = pl.BlockSpec((1, PEER_TOPK, tn), lambda h, i: (h, 0, i))
    return pl.pallas_call(
        _topk_kernel,
        grid=(heads, n // tn),
        in_specs=[blk_in, blk_in],
        out_specs=[blk_out, blk_out],
        out_shape=[jax.ShapeDtypeStruct((heads, PEER_TOPK, n), F32),
                   jax.ShapeDtypeStruct((heads, PEER_TOPK, n), jnp.int32)],
        compiler_params=_cparams("arbitrary", "arbitrary"),
        name="peer_topk",
    )(s1, s2)


def pack_expert_table(tab):
    n_exp, d = tab.shape
    bits = lax.bitcast_convert_type(tab.astype(BF16), jnp.uint16).astype(jnp.uint32)
    half = n_exp // 2
    words = (bits[:half] << 16) | bits[half:]
    return words.reshape(half, d // LANES, LANES)


def _reduce_plan():
    from itertools import combinations

    def build(groups, d):
        if d == SUBLANES:
            return ("leaf", groups[0][1])
        n = len(groups)
        for pick in combinations(range(n), n // 2):
            rows = set().union(*[groups[i][0] for i in pick])
            if all((r in rows) != (((r - d) % SUBLANES) in rows) for r in range(SUBLANES)):
                grow = lambda g: (g[0] | {(r - d) % SUBLANES for r in g[0]}, g[1])
                ga = [grow(g) for i, g in enumerate(groups) if i in pick]
                gb = [grow(g) for i, g in enumerate(groups) if i not in pick]
                return ("node", d, sum(1 << r for r in rows), build(ga, 2 * d), build(gb, 2 * d))
        raise AssertionError("no valid row mask")

    return build([({r}, r) for r in range(SUBLANES)], 1)


_REDUCE_TREE = _reduce_plan()


def _tree_masks(tree, sub, out):
    if tree[0] == "node":
        out[tree[2]] = ((tree[2] >> sub) & 1) == 1
        _tree_masks(tree[3], sub, out)
        _tree_masks(tree[4], sub, out)
    return out


def _tree_reduce(tree, leaf_fn, masks):
    if tree[0] == "leaf":
        return leaf_fn(tree[1])
    _, d, bits, left, right = tree
    a = _tree_reduce(left, leaf_fn, masks)
    b = _tree_reduce(right, leaf_fn, masks)
    m = masks[bits]
    return jnp.where(m, a, b) + pltpu.roll(jnp.where(m, b, a), d, 0)


def _lane_to_sublane_replicated(row, eye, ones, parts):
    diag = jnp.where(eye, jnp.broadcast_to(row, eye.shape), 0.0)
    out = None
    for _ in range(parts):
        term = diag.astype(BF16)
        prod = jnp.dot(term, ones, preferred_element_type=F32)
        out = prod if out is None else out + prod
        diag = diag - term.astype(F32)
    return out


def _expert_row(tab_ref, shift_ref, idx_ref, t, j):
    e = idx_ref[t, j]
    w = tab_ref[e & (PEER_HALF_EXPERTS - 1)]
    sh = jnp.broadcast_to(shift_ref[pl.ds(j, 1), :], (SUBLANES, LANES))
    return lax.bitcast_convert_type((w << sh) & jnp.uint32(0xFFFF0000), F32)


def _store_shift_rows(idx_row, eye, ones, shift_ref):
    shift = ((idx_row >> 13) << 4).astype(F32)
    rep = _lane_to_sublane_replicated(shift, eye, ones, 1)
    shift_ref[...] = rep.astype(jnp.int32).astype(jnp.uint32)


def _peer_u_kernel(idx_ref, idxv_ref, h_ref, tab_ref, r_ref, shift_scr, part_scr, *, tb):
    sub = lax.broadcasted_iota(jnp.int32, (SUBLANES, LANES), 0)
    masks = _tree_masks(_REDUCE_TREE, sub, {})
    eye = (lax.broadcasted_iota(jnp.int32, (LANES, LANES), 0)
           == lax.broadcasted_iota(jnp.int32, (LANES, LANES), 1))
    ones = jnp.ones((LANES, LANES), BF16)
    ones8 = jnp.ones((SUBLANES, LANES), BF16)
    nt = (((1,), (1,)), ((), ()))

    def group(gi, _):
        base = pl.multiple_of(gi * SUBLANES, SUBLANES)
        idx_tile = idxv_ref[pl.ds(base, SUBLANES), :]
        rows = []
        for i in range(SUBLANES):
            t = base + i
            hv = h_ref[t]
            _store_shift_rows(idx_tile[i:i + 1], eye, ones, shift_scr)

            def chunk(ci, _):
                j0 = pl.multiple_of(ci * SUBLANES, SUBLANES)
                leaf = lambda e: _expert_row(tab_ref, shift_scr, idx_ref, t, j0 + e) * hv
                part_scr[pl.ds(j0, SUBLANES), :] = _tree_reduce(_REDUCE_TREE, leaf, masks)
                return 0

            lax.fori_loop(0, PEER_SLOTS // SUBLANES, chunk, 0)
            part = part_scr[...]
            hi = part.astype(BF16)
            lo = (part - hi.astype(F32)).astype(BF16)
            tot = (lax.dot_general(ones8, hi, nt, preferred_element_type=F32)
                   + lax.dot_general(ones8, lo, nt, preferred_element_type=F32))
            rows.append(tot[0:1])
        r_ref[pl.ds(base, SUBLANES), :] = jnp.concatenate(rows, axis=0)
        return 0

    lax.fori_loop(0, tb // SUBLANES, group, 0)


def _peer_v_kernel(idx_ref, idxv_ref, r_ref, g_ref, x_ref, g2_ref, tab_ref, o_ref, shift_scr, coef_scr, *, tb):
    eye = (lax.broadcasted_iota(jnp.int32, (LANES, LANES), 0)
           == lax.broadcasted_iota(jnp.int32, (LANES, LANES), 1))
    ones = jnp.ones((LANES, LANES), BF16)
    g2 = g2_ref[0]
    n_acc = 4

    def group(gi, _):
        base = pl.multiple_of(gi * SUBLANES, SUBLANES)
        idx_tile = idxv_ref[pl.ds(base, SUBLANES), :]
        coef_tile = g_ref[pl.ds(base, SUBLANES), :] * jax.nn.gelu(r_ref[pl.ds(base, SUBLANES), :])
        for i in range(SUBLANES):
            t = base + i
            _store_shift_rows(idx_tile[i:i + 1], eye, ones, shift_scr)
            coef_scr[...] = _lane_to_sublane_replicated(coef_tile[i:i + 1], eye, ones, 3)

            def chunk(ci, accs):
                j0 = pl.multiple_of(ci * SUBLANES, SUBLANES)
                accs = list(accs)
                for e in range(SUBLANES):
                    row = _expert_row(tab_ref, shift_scr, idx_ref, t, j0 + e)
                    cf = jnp.broadcast_to(coef_scr[pl.ds(j0 + e, 1), :], (SUBLANES, LANES))
                    accs[e % n_acc] = accs[e % n_acc] + row * cf
                return tuple(accs)

            zero = jnp.zeros((SUBLANES, LANES), F32)
            accs = lax.fori_loop(0, PEER_SLOTS // SUBLANES, chunk, (zero,) * n_acc)
            mix = (accs[0] + accs[1]) + (accs[2] + accs[3])
            o_ref[t] = x_ref[t] + g2 * mix
        return 0

    lax.fori_loop(0, tb // SUBLANES, group, 0)


def peer_u_dots(idx, h2, u_words, seq, tb=256):
    n = idx.shape[0]
    tb = min(tb, seq)
    assert seq % tb == 0
    d_sub = D_MODEL // LANES
    return pl.pallas_call(
        functools.partial(_peer_u_kernel, tb=tb),
        grid=(n // tb,),
        in_specs=[
            pl.BlockSpec((tb, PEER_SLOTS), lambda i: (i, 0), memory_space=pltpu.SMEM),
            pl.BlockSpec((tb, PEER_SLOTS), lambda i: (i, 0)),
            pl.BlockSpec((tb, d_sub, LANES), lambda i: (i, 0, 0)),
            pl.BlockSpec(memory_space=pltpu.VMEM),
        ],
        out_specs=pl.BlockSpec((tb, PEER_SLOTS), lambda i: (i, 0)),
        out_shape=jax.ShapeDtypeStruct((n, PEER_SLOTS), F32),
        scratch_shapes=[pltpu.VMEM((PEER_SLOTS, LANES), jnp.uint32), pltpu.VMEM((PEER_SLOTS, LANES), F32)],
        compiler_params=_cparams("arbitrary"),
        name="peer_u_dots",
    )(idx, idx, h2.reshape(n, d_sub, LANES), u_words)


def peer_v_mix(idx, r, gate, x1, g2, v_words, seq, tb=256):
    n = idx.shape[0]
    tb = min(tb, seq)
    assert seq % tb == 0
    bsz = n // seq
    d_sub = D_MODEL // LANES
    tok = lambda i: (i, 0)
    tok3 = lambda i: (i, 0, 0)
    out = pl.pallas_call(
        functools.partial(_peer_v_kernel, tb=tb),
        grid=(n // tb,),
        in_specs=[
            pl.BlockSpec((tb, PEER_SLOTS), tok, memory_space=pltpu.SMEM),
            pl.BlockSpec((tb, PEER_SLOTS), tok),
            pl.BlockSpec((tb, PEER_SLOTS), tok),
            pl.BlockSpec((tb, PEER_SLOTS), tok),
            pl.BlockSpec((tb, d_sub, LANES), tok3),
            pl.BlockSpec((1, d_sub, LANES), lambda i: (i * tb // seq, 0, 0)),
            pl.BlockSpec(memory_space=pltpu.VMEM),
        ],
        out_specs=pl.BlockSpec((tb, d_sub, LANES), tok3),
        out_shape=jax.ShapeDtypeStruct((n, d_sub, LANES), F32),
        scratch_shapes=[pltpu.VMEM((PEER_SLOTS, LANES), jnp.uint32), pltpu.VMEM((PEER_SLOTS, LANES), F32)],
        compiler_params=_cparams("arbitrary"),
        name="peer_v_mix",
    )(idx, idx, r, gate, x1.reshape(n, d_sub, LANES), g2.reshape(bsz, d_sub, LANES), v_words)
    return out.reshape(n, D_MODEL)


def peer_block(x1, h2, s1, s2, g2, u_tab, v_tab, seq):
    n = x1.shape[0]
    gate, expert = peer_topk(s1, s2)
    slots = lambda a: a.transpose(2, 0, 1).reshape(n, PEER_SLOTS)
    idx = slots(expert)
    r = peer_u_dots(idx, h2, pack_expert_table(u_tab), seq)
    return peer_v_mix(idx, r, slots(gate), x1, g2, pack_expert_table(v_tab), seq)


def kernel(x, c, ada_w, ada_b, norm1_w, norm2_w, w_in, ssm_log_dt, ssm_a_re, ssm_a_im, ssm_b_re, ssm_b_im, ssm_c_re, ssm_c_im, ssm_d, ssm_glu_w, q_norm_w, k_norm_w, lambda_q1, lambda_k1, lambda_q2, lambda_k2, subln_w, attn_up_w, w_out, peer_wq, peer_k1, peer_k2, peer_u, peer_v):
    bsz, seq, d = x.shape
    n = bsz * seq
    depth = ada_w.shape[0]
    mod = ada_modulation(c, ada_w, ada_b)
    x2d = x.reshape(n, d)
    for l in range(depth):
        sh1, sc1, g1, sh2, sc2, g2 = jnp.split(mod[l], 6, axis=-1)
        u, q, k, v, gs, ga = in_projection(x2d, seq, norm1_w[l], sc1, sh1, w_in[l], q_norm_w[l], k_norm_w[l])
        mats = s5_matrices(ssm_log_dt[l], ssm_a_re[l], ssm_a_im[l], ssm_b_re[l], ssm_b_im[l],
                           ssm_c_re[l], ssm_c_im[l], ssm_d[l])
        yg = s5_mix(u.T, bsz, seq, *mats).T
        lambda_init = 0.8 - 0.6 * math.exp(-0.3 * l)
        lam = (jnp.exp(jnp.sum(lambda_q1[l] * lambda_k1[l])) - jnp.exp(jnp.sum(lambda_q2[l] * lambda_k2[l]))
               + lambda_init)
        o_att = diff_attention(q, k, v, seq, lam, subln_w[l], lambda_init)
        x1, h2, s1, s2 = mid_block(x2d, seq, yg, o_att, gs, ga, g1, sc2, sh2, norm2_w[l], ssm_glu_w[l],
                                   attn_up_w[l], w_out[l], peer_wq[l], peer_k1[l], peer_k2[l])
        x2d = peer_block(x1, h2, s1, s2, g2, peer_u[l], peer_v[l], seq)
    return x2d.reshape(bsz, seq, d)
```

```python
import functools
import math

import jax
import jax.numpy as jnp
import numpy as np
from jax import lax
from jax.experimental import pallas as pl
from jax.experimental.pallas import tpu as pltpu

F32 = jnp.float32
BF16 = jnp.bfloat16

D_MODEL = 1024
SSM_GROUPS = 32
SSM_GROUP_CH = 16
SSM_WIDTH = SSM_GROUPS * SSM_GROUP_CH
SSM_STATE = 64
ATTN_HEADS = 4
ATTN_HEAD_DIM = 64
ATTN_V_DIM = 2 * ATTN_HEAD_DIM
QK_WIDTH = ATTN_HEADS * 2 * ATTN_HEAD_DIM
ATTN_WIDTH = ATTN_HEADS * ATTN_V_DIM
PEER_HEADS = 8
PEER_N_KEYS = 128
PEER_HALF = 128
PEER_TOPK = 16
PEER_SLOTS = PEER_HEADS * PEER_TOPK
PEER_HALF_EXPERTS = PEER_N_KEYS * PEER_N_KEYS // 2
RMS_EPS = 1e-6
NEG_BIG = -1e30
LOG2E = 1.4426950408889634

LANES = 128
SUBLANES = 8
S5_CHUNK = LANES
VMEM_LIMIT = 56 * 1024 * 1024


def _cparams(*sem):
    return pltpu.CompilerParams(dimension_semantics=sem, vmem_limit_bytes=VMEM_LIMIT)


def _ada_kernel(c_ref, w_ref, b_ref, o_ref):
    c = c_ref[...]
    ca = c * jax.nn.sigmoid(c)
    o_ref[0] = jnp.dot(ca, w_ref[0], preferred_element_type=F32) + b_ref[0]


def ada_modulation(c, ada_w, ada_b):
    depth, d, d6 = ada_w.shape
    bsz = c.shape[0]
    rows = -(-bsz // SUBLANES) * SUBLANES
    c_pad = jnp.zeros((rows, d), F32).at[:bsz].set(c)
    out = pl.pallas_call(
        _ada_kernel,
        grid=(depth, d6 // d),
        in_specs=[
            pl.BlockSpec((rows, d), lambda l, j: (0, 0)),
            pl.BlockSpec((1, d, d), lambda l, j: (l, 0, j)),
            pl.BlockSpec((1, 1, d), lambda l, j: (l, 0, j)),
        ],
        out_specs=pl.BlockSpec((1, rows, d), lambda l, j: (l, 0, j)),
        out_shape=jax.ShapeDtypeStruct((depth, rows, d6), F32),
        compiler_params=_cparams("arbitrary", "arbitrary"),
        name="ada_modulation",
    )(c_pad, ada_w, ada_b.reshape(depth, 1, d6))
    return out[:, :bsz]


def _group_norm64(z, gmat, w_row):
    ssq = jnp.dot((z * z).astype(BF16), gmat, preferred_element_type=F32)
    return z * lax.rsqrt(ssq * (1.0 / ATTN_HEAD_DIM) + RMS_EPS) * w_row


def _inproj_kernel(x_ref, nw_ref, sc_ref, sh_ref, w_ref, qw_ref, kw_ref, gmat_ref,
                   u_ref, q_ref, k_ref, v_ref, gs_ref, ga_ref):
    x = x_ref[...]
    ms = jnp.mean(x * x, axis=-1, keepdims=True)
    h = x * lax.rsqrt(ms + RMS_EPS) * nw_ref[...]
    h = h * (1.0 + sc_ref[0]) + sh_ref[0]
    hb = h.astype(BF16)
    o = 0
    u_ref[...] = jnp.dot(hb, w_ref[:, o:o + SSM_WIDTH], preferred_element_type=F32).astype(BF16)
    o += SSM_WIDTH
    q = jnp.dot(hb, w_ref[:, o:o + QK_WIDTH], preferred_element_type=F32)
    q_ref[...] = _group_norm64(q, gmat_ref[...], qw_ref[...]).astype(BF16)
    o += QK_WIDTH
    k = jnp.dot(hb, w_ref[:, o:o + QK_WIDTH], preferred_element_type=F32)
    k_ref[...] = _group_norm64(k, gmat_ref[...], kw_ref[...]).astype(BF16)
    o += QK_WIDTH
    v_ref[...] = jnp.dot(hb, w_ref[:, o:o + ATTN_WIDTH], preferred_element_type=F32).astype(BF16)
    o += ATTN_WIDTH
    gs = jnp.dot(hb, w_ref[:, o:o + D_MODEL], preferred_element_type=F32)
    gs_ref[...] = jax.nn.sigmoid(gs).astype(BF16)
    o += D_MODEL
    ga = jnp.dot(hb, w_ref[:, o:o + D_MODEL], preferred_element_type=F32)
    ga_ref[...] = jax.nn.sigmoid(ga).astype(BF16)


def in_projection(x2d, seq, norm_w, sc, sh, w_in, q_norm_w, k_norm_w, tm=512):
    n, d = x2d.shape
    tm = min(tm, seq)
    assert seq % tm == 0 and n % tm == 0
    bsz = n // seq
    width = w_in.shape[1]
    qw = jnp.tile(q_norm_w, QK_WIDTH // ATTN_HEAD_DIM)[None] * (ATTN_HEAD_DIM ** -0.5 * LOG2E)
    kw = jnp.tile(k_norm_w, QK_WIDTH // ATTN_HEAD_DIM)[None]
    grp = np.arange(QK_WIDTH) // ATTN_HEAD_DIM
    gmat = jnp.asarray(grp[:, None] == grp[None, :], BF16)
    row = lambda i: (i, 0)
    fixed = lambda i: (0, 0)
    per_b = lambda i: (i * tm // seq, 0, 0)
    out_w = (SSM_WIDTH, QK_WIDTH, QK_WIDTH, ATTN_WIDTH, D_MODEL, D_MODEL)
    return pl.pallas_call(
        _inproj_kernel,
        grid=(n // tm,),
        in_specs=[
            pl.BlockSpec((tm, d), row),
            pl.BlockSpec((1, d), fixed),
            pl.BlockSpec((1, 1, d), per_b),
            pl.BlockSpec((1, 1, d), per_b),
            pl.BlockSpec((d, width), fixed),
            pl.BlockSpec((1, QK_WIDTH), fixed),
            pl.BlockSpec((1, QK_WIDTH), fixed),
            pl.BlockSpec((QK_WIDTH, QK_WIDTH), fixed),
        ],
        out_specs=[pl.BlockSpec((tm, w), row) for w in out_w],
        out_shape=[jax.ShapeDtypeStruct((n, w), BF16) for w in out_w],
        compiler_params=_cparams("arbitrary"),
        name="in_projection",
    )(x2d, norm_w[None], sc.reshape(bsz, 1, d), sh.reshape(bsz, 1, d), w_in.astype(BF16), qw, kw, gmat)


def _attn_kernel(lam_ref, q_ref, k_ref, v_ref, sw_ref, o_ref, qq_scr, m_scr, acc_scr, *, blk, rows, out_scale):
    qi = pl.program_id(2)
    q = q_ref[...]
    lane = lax.broadcasted_iota(jnp.int32, q.shape, 1)
    zero = jnp.zeros_like(q)
    qq_scr[...] = jnp.concatenate([jnp.where(lane < ATTN_HEAD_DIM, q, zero),
                                   jnp.where(lane >= ATTN_HEAD_DIM, q, zero)], axis=0)
    ones = jnp.ones((blk, LANES), BF16)

    m_scr[...] = jnp.full(m_scr.shape, NEG_BIG, F32)
    acc_scr[...] = jnp.zeros(acc_scr.shape, F32)

    def step(ki, masked):
        off = pl.multiple_of(ki * blk, blk)
        kb = k_ref[pl.ds(off, blk), :]
        vb = jnp.concatenate([v_ref[pl.ds(off, blk), :], ones], axis=1)
        for r0 in range(0, 2 * blk, rows):
            rs = pl.ds(r0, rows)
            s = lax.dot_general(qq_scr[rs, :], kb, (((1,), (1,)), ((), ())), preferred_element_type=F32)
            if masked:
                row = lax.broadcasted_iota(jnp.int32, s.shape, 0) + (r0 % blk)
                col = lax.broadcasted_iota(jnp.int32, s.shape, 1)
                s = jnp.where(col <= row, s, NEG_BIG)
            m = m_scr[rs, :]
            m_new = jnp.maximum(m, jnp.max(s, axis=1, keepdims=True))
            alpha = jnp.exp2(m - m_new)
            p = jnp.exp2(s - m_new).astype(BF16)
            acc_scr[rs, :] = alpha * acc_scr[rs, :] + jnp.dot(p, vb, preferred_element_type=F32)
            m_scr[rs, :] = m_new

    @pl.loop(0, qi)
    def _(ki):
        step(ki, False)

    step(qi, True)
    acc = acc_scr[...]
    o_all = acc[:, :LANES] / acc[:, LANES:]
    o = o_all[:blk] - lam_ref[0] * o_all[blk:]
    ms = jnp.mean(o * o, axis=-1, keepdims=True)
    o_ref[...] = (o * lax.rsqrt(ms + RMS_EPS) * sw_ref[...] * out_scale).astype(BF16)


def diff_attention(q, k, v, seq, lam, subln_w, lambda_init, blk=1024, rows=128):
    n = q.shape[0]
    blk = min(blk, seq)
    rows = min(rows, blk)
    assert seq % blk == 0 and blk % rows == 0
    bsz = n // seq
    nq = seq // blk
    return pl.pallas_call(
        functools.partial(_attn_kernel, blk=blk, rows=rows, out_scale=1.0 - lambda_init),
        grid=(bsz, ATTN_HEADS, nq),
        in_specs=[
            pl.BlockSpec(memory_space=pltpu.SMEM),
            pl.BlockSpec((blk, ATTN_V_DIM), lambda b, h, i: (b * nq + i, h)),
            pl.BlockSpec((seq, ATTN_V_DIM), lambda b, h, i: (b, h)),
            pl.BlockSpec((seq, ATTN_V_DIM), lambda b, h, i: (b, h)),
            pl.BlockSpec((1, ATTN_V_DIM), lambda b, h, i: (0, 0)),
        ],
        out_specs=pl.BlockSpec((blk, ATTN_V_DIM), lambda b, h, i: (b * nq + i, h)),
        out_shape=jax.ShapeDtypeStruct((n, ATTN_WIDTH), BF16),
        scratch_shapes=[pltpu.VMEM((2 * blk, ATTN_V_DIM), BF16), pltpu.VMEM((2 * blk, 1), F32),
                        pltpu.VMEM((2 * blk, 2 * LANES), F32)],
        compiler_params=_cparams("arbitrary", "arbitrary", "arbitrary"),
        name="diff_attention",
    )(lam.reshape(1), q, k, v, subln_w[None])


def s5_matrices(log_dt, a_re, a_im, b_re, b_im, c_re, c_im, d_skip):
    t_len = S5_CHUNK
    g, p, h = b_re.shape
    dt = jnp.exp(log_dt)[:, None]
    mag = jnp.exp(a_re * dt)
    ab_re = mag * jnp.cos(a_im * dt)
    ab_im = mag * jnp.sin(a_im * dt)
    den = a_re * a_re + a_im * a_im
    nr = ab_re - 1.0
    f_re = (nr * a_re + ab_im * a_im) / den
    f_im = (ab_im * a_re - nr * a_im) / den
    bb_re = f_re[..., None] * b_re - f_im[..., None] * b_im
    bb_im = f_re[..., None] * b_im + f_im[..., None] * b_re
    tau = jnp.arange(t_len + 1, dtype=F32)[None, :, None]
    pmag = jnp.exp(a_re[:, None, :] * dt[:, None, :] * tau)
    pw_re = pmag * jnp.cos(a_im[:, None, :] * dt[:, None, :] * tau)
    pw_im = pmag * jnp.sin(a_im[:, None, :] * dt[:, None, :] * tau)
    ca_re = c_re[:, None] * pw_re[:, :, None, :] - c_im[:, None] * pw_im[:, :, None, :]
    ca_im = c_re[:, None] * pw_im[:, :, None, :] + c_im[:, None] * pw_re[:, :, None, :]
    kern = (jnp.einsum("gtop,gpi->gtoi", ca_re[:, :t_len], bb_re)
            - jnp.einsum("gtop,gpi->gtoi", ca_im[:, :t_len], bb_im))
    kern = kern.at[:, 0].add(jax.vmap(jnp.diag)(d_skip))
    s_idx = jnp.arange(t_len)[:, None]
    t_idx = jnp.arange(t_len)[None, :]
    lag = t_idx - s_idx
    toep = jnp.where((lag >= 0)[None, :, :, None, None], kern[:, jnp.maximum(lag, 0)], 0.0)
    toep = toep.transpose(0, 4, 1, 3, 2).reshape(g, h * t_len, h * t_len)
    rp_re = pw_re[:, t_len - 1::-1][:, :t_len]
    rp_im = pw_im[:, t_len - 1::-1][:, :t_len]
    w_re = rp_re[..., None] * bb_re[:, None] - rp_im[..., None] * bb_im[:, None]
    w_im = rp_re[..., None] * bb_im[:, None] + rp_im[..., None] * bb_re[:, None]
    w_st = jnp.concatenate([w_re, w_im], axis=2).transpose(0, 3, 1, 2).reshape(g, h * t_len, 2 * p)
    v_st = jnp.concatenate([ca_re[:, 1:], -ca_im[:, 1:]], axis=3)
    v_st = v_st.transpose(0, 3, 2, 1).reshape(g, 2 * p, h * t_len)
    at_re, at_im = pw_re[:, t_len], pw_im[:, t_len]
    coef = jnp.stack([jnp.concatenate([at_re, at_re], -1), jnp.concatenate([-at_im, at_im], -1),
                      jnp.concatenate([at_im, -at_im], -1)], axis=1)
    return toep.astype(BF16), w_st.astype(BF16), v_st.astype(BF16), coef


def _s5_kernel(u_ref, toep_ref, w_ref, v_ref, coef_ref, y_ref, s_scr, sw_scr, xp_scr, *, bsz, nchunk):
    hch = u_ref.shape[1]
    xin = jnp.concatenate([u_ref[0, i] for i in range(hch)], axis=1)
    y = jnp.dot(xin, toep_ref[0], preferred_element_type=F32)
    s = jnp.dot(xin, w_ref[0], preferred_element_type=F32)
    half = s.shape[1] // 2
    s_scr[...] = s
    sw_scr[...] = pltpu.roll(s, half, 1)
    coef = coef_ref[0]
    a_cat, b_cat, b_swp = coef[0:1], coef[1:2], coef[2:3]

    def body(c8, carry):
        new = []
        for b in range(bsz):
            x, xs = carry[2 * b], carry[2 * b + 1]
            base = pl.multiple_of(b * nchunk + c8 * SUBLANES, SUBLANES)
            st = s_scr[pl.ds(base, SUBLANES), :]
            sw = sw_scr[pl.ds(base, SUBLANES), :]
            prev = []
            for i in range(SUBLANES):
                prev.append(x)
                x, xs = (a_cat * x + b_cat * xs + st[i:i + 1], a_cat * xs + b_swp * x + sw[i:i + 1])
            xp_scr[pl.ds(base, SUBLANES), :] = jnp.concatenate(prev, axis=0)
            new += [x, xs]
        return tuple(new)

    zero = jnp.zeros((1, s.shape[1]), F32)
    lax.fori_loop(0, nchunk // SUBLANES, body, (zero,) * (2 * bsz))
    xp = xp_scr[...]
    xp_hi = xp.astype(BF16)
    xp_lo = (xp - xp_hi.astype(F32)).astype(BF16)
    y = y + jnp.dot(xp_hi, v_ref[0], preferred_element_type=F32) + jnp.dot(xp_lo, v_ref[0], preferred_element_type=F32)
    y = jax.nn.gelu(y)
    t_len = y.shape[1] // hch
    for i in range(hch):
        y_ref[0, i] = y[:, i * t_len:(i + 1) * t_len].astype(BF16)


def s5_mix(u_t, bsz, seq, toep, w_st, v_st, coef):
    t_len = S5_CHUNK
    n = bsz * seq
    assert seq % (t_len * SUBLANES) == 0
    nchunk = seq // t_len
    rows = n // t_len
    g, hch = SSM_GROUPS, SSM_GROUP_CH
    u4 = u_t.reshape(g, hch, rows, t_len)
    kt = hch * t_len
    p2 = 2 * SSM_STATE
    grp = lambda i: (i, 0, 0)
    y4 = pl.pallas_call(
        functools.partial(_s5_kernel, bsz=bsz, nchunk=nchunk),
        grid=(g,),
        in_specs=[
            pl.BlockSpec((1, hch, rows, t_len), lambda i: (i, 0, 0, 0)),
            pl.BlockSpec((1, kt, kt), grp),
            pl.BlockSpec((1, kt, p2), grp),
            pl.BlockSpec((1, p2, kt), grp),
            pl.BlockSpec((1, 3, p2), grp),
        ],
        out_specs=pl.BlockSpec((1, hch, rows, t_len), lambda i: (i, 0, 0, 0)),
        out_shape=jax.ShapeDtypeStruct((g, hch, rows, t_len), BF16),
        scratch_shapes=[pltpu.VMEM((rows, p2), F32)] * 3,
        compiler_params=_cparams("arbitrary"),
        name="s5_mix",
    )(u4, toep, w_st, v_st, coef)
    return y4.reshape(g * hch, n)


def _mid_kernel(x_ref, yg_ref, oa_ref, gs_ref, ga_ref, g1_ref, sc2_ref, sh2_ref, nw2_ref,
                glu_ref, up_ref, wo_ref, wq_ref, k1_ref, k2_ref,
                x1_ref, h2_ref, s1_ref, s2_ref):
    vg = jnp.dot(yg_ref[...], glu_ref[...], preferred_element_type=F32)
    y_ssm = vg[:, :D_MODEL] * jax.nn.sigmoid(vg[:, D_MODEL:])
    y_att = jnp.dot(oa_ref[...], up_ref[...], preferred_element_type=F32)
    merged = gs_ref[...].astype(F32) * y_ssm + ga_ref[...].astype(F32) * y_att
    x1 = x_ref[...] + g1_ref[0] * jnp.dot(merged.astype(BF16), wo_ref[...], preferred_element_type=F32)
    x1_ref[...] = x1
    ms = jnp.mean(x1 * x1, axis=-1, keepdims=True)
    h2 = x1 * lax.rsqrt(ms + RMS_EPS) * nw2_ref[...]
    h2 = h2 * (1.0 + sc2_ref[0]) + sh2_ref[0]
    h2_ref[...] = h2
    qp = jnp.dot(h2.astype(BF16), wq_ref[...], preferred_element_type=F32).astype(BF16)
    nt = (((1,), (1,)), ((), ()))
    for hh in range(PEER_HEADS):
        o = hh * 2 * PEER_HALF
        s1_ref[hh] = lax.dot_general(k1_ref[...], qp[:, o:o + PEER_HALF], nt, preferred_element_type=F32)
        s2_ref[hh] = lax.dot_general(k2_ref[...], qp[:, o + PEER_HALF:o + 2 * PEER_HALF], nt,
                                     preferred_element_type=F32)


def mid_block(x2d, seq, yg, o_att, gs, ga, g1, sc2, sh2, norm2_w, glu_w, up_w, w_out, peer_wq, k1, k2, tm=256):
    n, d = x2d.shape
    tm = min(tm, seq)
    assert seq % tm == 0
    bsz = n // seq
    row = lambda i: (i, 0)
    fixed = lambda i: (0, 0)
    per_b = lambda i: (i * tm // seq, 0, 0)
    full = lambda a: pl.BlockSpec(a.shape, fixed)
    weights = [w.astype(BF16) for w in (glu_w, up_w, w_out, peer_wq, k1, k2)]
    sc_shape = (PEER_HEADS, PEER_N_KEYS, n)
    return pl.pallas_call(
        _mid_kernel,
        grid=(n // tm,),
        in_specs=[
            pl.BlockSpec((tm, d), row),
            pl.BlockSpec((tm, SSM_WIDTH), row),
            pl.BlockSpec((tm, ATTN_WIDTH), row),
            pl.BlockSpec((tm, d), row),
            pl.BlockSpec((tm, d), row),
            pl.BlockSpec((1, 1, d), per_b),
            pl.BlockSpec((1, 1, d), per_b),
            pl.BlockSpec((1, 1, d), per_b),
            pl.BlockSpec((1, d), fixed),
        ] + [full(w) for w in weights],
        out_specs=[
            pl.BlockSpec((tm, d), row),
            pl.BlockSpec((tm, d), row),
            pl.BlockSpec((PEER_HEADS, PEER_N_KEYS, tm), lambda i: (0, 0, i)),
            pl.BlockSpec((PEER_HEADS, PEER_N_KEYS, tm), lambda i: (0, 0, i)),
        ],
        out_shape=[jax.ShapeDtypeStruct((n, d), F32), jax.ShapeDtypeStruct((n, d), F32),
                   jax.ShapeDtypeStruct(sc_shape, F32), jax.ShapeDtypeStruct(sc_shape, F32)],
        compiler_params=_cparams("arbitrary"),
        name="mid_block",
    )(x2d, yg, o_att, gs, ga, g1.reshape(bsz, 1, d), sc2.reshape(bsz, 1, d), sh2.reshape(bsz, 1, d),
      norm2_w[None], *weights)


def _top16_rows(s):
    nrows = float(s.shape[0])
    rowid = lax.broadcasted_iota(jnp.int32, s.shape, 0).astype(F32)
    vals, ids = [], []
    for _ in range(PEER_TOPK):
        m = jnp.max(s, axis=0, keepdims=True)
        i = jnp.min(jnp.where(s == m, rowid, nrows), axis=0, keepdims=True)
        vals.append(m)
        ids.append(i)
        s = jnp.where(rowid == i, -jnp.inf, s)
    return jnp.concatenate(vals, axis=0), jnp.concatenate(ids, axis=0)


def _topk_kernel(s1_ref, s2_ref, g_ref, e_ref, *, width):
    def tile(ti, _):
        cols = pl.ds(pl.multiple_of(ti * width, width), width)
        gate, expert = _topk_tile(s1_ref[0, :, cols], s2_ref[0, :, cols])
        g_ref[0, :, cols] = gate
        e_ref[0, :, cols] = expert
        return 0

    lax.fori_loop(0, s1_ref.shape[2] // width, tile, 0)


def _topk_tile(s1, s2):
    v1, i1 = _top16_rows(s1)
    v2, i2 = _top16_rows(s2)
    half = PEER_TOPK // 2
    sub = lax.broadcasted_iota(jnp.int32, (half, v1.shape[1]), 0).astype(F32)
    cand, pay, flat = [], [], []
    for i in range(half):
        cand.append(v1[i:i + 1] + v2[:half])
        pay.append(i1[i:i + 1] * PEER_N_KEYS + i2[:half])
        flat.append(sub + i * PEER_TOPK)
    cand.append(v1[0:1] + v2[half:])
    pay.append(i1[0:1] * PEER_N_KEYS + i2[half:])
    flat.append(sub + half)
    cand.append(v1[half:] + v2[0:1])
    pay.append(i1[half:] * PEER_N_KEYS + i2[0:1])
    flat.append((sub + half) * PEER_TOPK)
    cand = jnp.concatenate(cand, axis=0)
    pay = jnp.concatenate(pay, axis=0)
    flat = jnp.concatenate(flat, axis=0)
    big = float(PEER_TOPK * PEER_TOPK)
    sc, ex = [], []
    for _ in range(PEER_TOPK):
        m = jnp.max(cand, axis=0, keepdims=True)
        f = jnp.min(jnp.where(cand == m, flat, big), axis=0, keepdims=True)
        sel = flat == f
        ex.append(jnp.max(jnp.where(sel, pay, -1.0), axis=0, keepdims=True))
        sc.append(m)
        cand = jnp.where(sel, -jnp.inf, cand)
    sc = jnp.concatenate(sc, axis=0)
    p = jnp.exp(sc - sc[0:1])
    return p / jnp.sum(p, axis=0, keepdims=True), jnp.concatenate(ex, axis=0).astype(jnp.int32)


def peer_topk(s1, s2, tn=2048, width=2 * LANES):
    heads, nkeys, n = s1.shape
    tn = min(tn, n)
    width = min(width, tn)
    assert n % tn == 0 and tn % width == 0
    blk_in = pl.BlockSpec((1, nkeys, tn), lambda h, i: (h, 0, i))
    blk_out = pl.BlockSpec((1, PEER_TOPK, tn), lambda h, i: (h, 0, i))
    return pl.pallas_call(
        functools.partial(_topk_kernel, width=width),
        grid=(heads, n // tn),
        in_specs=[blk_in, blk_in],
        out_specs=[blk_out, blk_out],
        out_shape=[jax.ShapeDtypeStruct((heads, PEER_TOPK, n), F32),
                   jax.ShapeDtypeStruct((heads, PEER_TOPK, n), jnp.int32)],
        compiler_params=_cparams("arbitrary", "arbitrary"),
        name="peer_topk",
    )(s1, s2)


PACK_ROWS = 2 * SUBLANES
EXPANDED = PEER_SLOTS * PACK_ROWS


def pack_expert_pairs(tab):
    n_exp, d = tab.shape
    half = n_exp // 2
    bits = lax.bitcast_convert_type(tab.astype(BF16), jnp.uint16).astype(jnp.uint32)
    words = (bits[:half] << 16) | bits[half:]
    return words.reshape(half * (d // LANES), LANES)


def _expand_matrix():
    k = np.arange(EXPANDED)
    return jnp.asarray((k[None, :] // PACK_ROWS) == np.arange(PEER_SLOTS)[:, None], BF16)


def _split2(x):
    hi = x.astype(BF16)
    lo = (x - hi.astype(F32)).astype(BF16)
    return hi, lo


def _gathered(tab_ref, row_ref, t):
    tok = row_ref.at[pl.ds(pl.multiple_of(t * PEER_SLOTS, PEER_SLOTS), PEER_SLOTS)]
    tiles = []
    for j in range(PEER_SLOTS):
        words = tab_ref[pl.ds(pl.multiple_of(tok[j], SUBLANES), SUBLANES), :]
        tiles.append(pltpu.bitcast(words, BF16))
    return jnp.concatenate(tiles, axis=0)


def _row_target(half_b):
    sub = lax.broadcasted_iota(jnp.int32, half_b.shape, 0).astype(F32)
    return half_b + 2.0 * sub


def _peer_u_kernel(row_ref, half_ref, h_ref, exp_ref, col_ref, tab_ref, r_ref, x_scr, m_scr, *, tb):
    kmod = (lax.broadcasted_iota(jnp.int32, (SUBLANES, EXPANDED), 1) % PACK_ROWS).astype(F32)
    nt = (((1,), (1,)), ((), ()))

    def group(gi, _):
        base = pl.multiple_of(gi * SUBLANES, SUBLANES)
        x_scr[...] = jnp.dot(half_ref[pl.ds(base, SUBLANES), :].astype(BF16), exp_ref[...],
                             preferred_element_type=F32)

        def token(i, _):
            t = base + i
            hh = jnp.concatenate(_split2(h_ref[t]), axis=0)
            w = _gathered(tab_ref, row_ref, t)
            s = lax.dot_general(hh, w, nt, preferred_element_type=F32)
            both = s[:SUBLANES] + s[SUBLANES:]
            half_b = jnp.broadcast_to(x_scr[pl.ds(i, 1), :], (SUBLANES, EXPANDED))
            m_scr[i] = jnp.where(kmod == _row_target(half_b), both, 0.0)
            return 0

        lax.fori_loop(0, SUBLANES, token, 0, unroll=4)
        m = m_scr[...].reshape(SUBLANES * SUBLANES, EXPANDED)
        tot = jnp.dot(jnp.concatenate(_split2(m), axis=0), col_ref[...], preferred_element_type=F32)
        tot = tot[:SUBLANES * SUBLANES] + tot[SUBLANES * SUBLANES:]
        r_ref[pl.ds(base, SUBLANES), :] = jnp.sum(tot.reshape(SUBLANES, SUBLANES, LANES), axis=1)
        return 0

    lax.fori_loop(0, tb // SUBLANES, group, 0)


def _peer_v_kernel(row_ref, half_ref, r_ref, g_ref, x_ref, g2_ref, exp_ref, tab_ref, o_ref, x_scr, *, tb):
    g2 = g2_ref[0]
    kmod = (lax.broadcasted_iota(jnp.int32, (SUBLANES, EXPANDED), 1) % PACK_ROWS).astype(F32)

    def group(gi, _):
        base = pl.multiple_of(gi * SUBLANES, SUBLANES)
        coef = g_ref[pl.ds(base, SUBLANES), :] * jax.nn.gelu(r_ref[pl.ds(base, SUBLANES), :])
        c_hi, c_lo = _split2(coef)
        lhs = jnp.concatenate([c_hi, c_lo, half_ref[pl.ds(base, SUBLANES), :].astype(BF16)], axis=0)
        x_scr[...] = jnp.dot(lhs, exp_ref[...], preferred_element_type=F32)

        def token(i, _):
            t = base + i
            bc = lambda r: jnp.broadcast_to(x_scr[pl.ds(r, 1), :], (SUBLANES, EXPANDED))
            sel = kmod == _row_target(bc(2 * SUBLANES + i))
            l_hi = jnp.where(sel, bc(i), 0.0)
            l_lo = jnp.where(sel, bc(SUBLANES + i), 0.0)
            lmat = jnp.concatenate([l_hi, l_lo], axis=0).astype(BF16)
            out = jnp.dot(lmat, _gathered(tab_ref, row_ref, t), preferred_element_type=F32)
            o_ref[t] = x_ref[t] + g2 * (out[:SUBLANES] + out[SUBLANES:])
            return 0

        lax.fori_loop(0, SUBLANES, token, 0, unroll=8)
        return 0

    lax.fori_loop(0, tb // SUBLANES, group, 0)


def peer_u_dots(rows, half, h2, u_words, seq, tb=256):
    n = half.shape[0]
    tb = min(tb, seq)
    assert seq % tb == 0
    d_sub = D_MODEL // LANES
    expand = _expand_matrix()
    return pl.pallas_call(
        functools.partial(_peer_u_kernel, tb=tb),
        grid=(n // tb,),
        in_specs=[
            pl.BlockSpec((tb * PEER_SLOTS,), lambda i: (i,), memory_space=pltpu.SMEM),
            pl.BlockSpec((tb, PEER_SLOTS), lambda i: (i, 0)),
            pl.BlockSpec((tb, d_sub, LANES), lambda i: (i, 0, 0)),
            pl.BlockSpec((PEER_SLOTS, EXPANDED), lambda i: (0, 0)),
            pl.BlockSpec((EXPANDED, PEER_SLOTS), lambda i: (0, 0)),
            pl.BlockSpec(memory_space=pltpu.VMEM),
        ],
        out_specs=pl.BlockSpec((tb, PEER_SLOTS), lambda i: (i, 0)),
        out_shape=jax.ShapeDtypeStruct((n, PEER_SLOTS), F32),
        scratch_shapes=[pltpu.VMEM((SUBLANES, EXPANDED), F32), pltpu.VMEM((SUBLANES, SUBLANES, EXPANDED), F32)],
        compiler_params=_cparams("arbitrary"),
        name="peer_u_dots",
    )(rows, half, h2.reshape(n, d_sub, LANES), expand, expand.T, u_words)


def peer_v_mix(rows, half, r, gate, x1, g2, v_words, seq, tb=256):
    n = half.shape[0]
    tb = min(tb, seq)
    assert seq % tb == 0
    bsz = n // seq
    d_sub = D_MODEL // LANES
    tok = lambda i: (i, 0)
    tok3 = lambda i: (i, 0, 0)
    out = pl.pallas_call(
        functools.partial(_peer_v_kernel, tb=tb),
        grid=(n // tb,),
        in_specs=[
            pl.BlockSpec((tb * PEER_SLOTS,), lambda i: (i,), memory_space=pltpu.SMEM),
            pl.BlockSpec((tb, PEER_SLOTS), tok),
            pl.BlockSpec((tb, PEER_SLOTS), tok),
            pl.BlockSpec((tb, PEER_SLOTS), tok),
            pl.BlockSpec((tb, d_sub, LANES), tok3),
            pl.BlockSpec((1, d_sub, LANES), lambda i: (i * tb // seq, 0, 0)),
            pl.BlockSpec((PEER_SLOTS, EXPANDED), lambda i: (0, 0)),
            pl.BlockSpec(memory_space=pltpu.VMEM),
        ],
        out_specs=pl.BlockSpec((tb, d_sub, LANES), tok3),
        out_shape=jax.ShapeDtypeStruct((n, d_sub, LANES), F32),
        scratch_shapes=[pltpu.VMEM((3 * SUBLANES, EXPANDED), F32)],
        compiler_params=_cparams("arbitrary"),
        name="peer_v_mix",
    )(rows, half, r, gate, x1.reshape(n, d_sub, LANES), g2.reshape(bsz, d_sub, LANES), _expand_matrix(), v_words)
    return out.reshape(n, D_MODEL)


def peer_block(x1, h2, s1, s2, g2, u_tab, v_tab, seq):
    n = x1.shape[0]
    gate, expert = peer_topk(s1, s2)
    slots = lambda a: a.transpose(2, 0, 1).reshape(n, PEER_SLOTS)
    idx = slots(expert)
    rows = ((idx & (PEER_HALF_EXPERTS - 1)) * SUBLANES).reshape(-1)
    half = (1 - idx // PEER_HALF_EXPERTS).astype(F32)
    r = peer_u_dots(rows, half, h2, pack_expert_pairs(u_tab), seq)
    return peer_v_mix(rows, half, r, slots(gate), x1, g2, pack_expert_pairs(v_tab), seq)


def kernel(x, c, ada_w, ada_b, norm1_w, norm2_w, w_in, ssm_log_dt, ssm_a_re, ssm_a_im, ssm_b_re, ssm_b_im, ssm_c_re, ssm_c_im, ssm_d, ssm_glu_w, q_norm_w, k_norm_w, lambda_q1, lambda_k1, lambda_q2, lambda_k2, subln_w, attn_up_w, w_out, peer_wq, peer_k1, peer_k2, peer_u, peer_v):
    bsz, seq, d = x.shape
    n = bsz * seq
    depth = ada_w.shape[0]
    mod = ada_modulation(c, ada_w, ada_b)
    x2d = x.reshape(n, d)
    for l in range(depth):
        sh1, sc1, g1, sh2, sc2, g2 = jnp.split(mod[l], 6, axis=-1)
        u, q, k, v, gs, ga = in_projection(x2d, seq, norm1_w[l], sc1, sh1, w_in[l], q_norm_w[l], k_norm_w[l])
        mats = s5_matrices(ssm_log_dt[l], ssm_a_re[l], ssm_a_im[l], ssm_b_re[l], ssm_b_im[l],
                           ssm_c_re[l], ssm_c_im[l], ssm_d[l])
        yg = s5_mix(u.T, bsz, seq, *mats).T
        lambda_init = 0.8 - 0.6 * math.exp(-0.3 * l)
        lam = (jnp.exp(jnp.sum(lambda_q1[l] * lambda_k1[l])) - jnp.exp(jnp.sum(lambda_q2[l] * lambda_k2[l]))
               + lambda_init)
        o_att = diff_attention(q, k, v, seq, lam, subln_w[l], lambda_init)
        x1, h2, s1, s2 = mid_block(x2d, seq, yg, o_att, gs, ga, g1, sc2, sh2, norm2_w[l], ssm_glu_w[l],
                                   attn_up_w[l], w_out[l], peer_wq[l], peer_k1[l], peer_k2[l])
        x2d = peer_block(x1, h2, s1, s2, g2, peer_u[l], peer_v[l], seq)
    return x2d.reshape(bsz, seq, d)
```

```python
import functools
import math

import jax
import jax.numpy as jnp
import numpy as np
from jax import lax
from jax.experimental import pallas as pl
from jax.experimental.pallas import tpu as pltpu

F32 = jnp.float32
BF16 = jnp.bfloat16

D_MODEL = 1024
SSM_GROUPS = 32
SSM_GROUP_CH = 16
SSM_WIDTH = SSM_GROUPS * SSM_GROUP_CH
SSM_STATE = 64
ATTN_HEADS = 4
ATTN_HEAD_DIM = 64
ATTN_V_DIM = 2 * ATTN_HEAD_DIM
QK_WIDTH = ATTN_HEADS * 2 * ATTN_HEAD_DIM
ATTN_WIDTH = ATTN_HEADS * ATTN_V_DIM
PEER_HEADS = 8
PEER_N_KEYS = 128
PEER_HALF = 128
PEER_TOPK = 16
PEER_SLOTS = PEER_HEADS * PEER_TOPK
PEER_HALF_EXPERTS = PEER_N_KEYS * PEER_N_KEYS // 2
RMS_EPS = 1e-6
NEG_BIG = -1e30
LOG2E = 1.4426950408889634

LANES = 128
SUBLANES = 8
S5_CHUNK = LANES
VMEM_LIMIT = 56 * 1024 * 1024


def _cparams(*sem):
    return pltpu.CompilerParams(dimension_semantics=sem, vmem_limit_bytes=VMEM_LIMIT)


def _ada_kernel(c_ref, w_ref, b_ref, o_ref):
    c = c_ref[...]
    ca = c * jax.nn.sigmoid(c)
    o_ref[0] = jnp.dot(ca, w_ref[0], preferred_element_type=F32) + b_ref[0]


def ada_modulation(c, ada_w, ada_b):
    depth, d, d6 = ada_w.shape
    bsz = c.shape[0]
    rows = -(-bsz // SUBLANES) * SUBLANES
    c_pad = jnp.zeros((rows, d), F32).at[:bsz].set(c)
    out = pl.pallas_call(
        _ada_kernel,
        grid=(depth, d6 // d),
        in_specs=[
            pl.BlockSpec((rows, d), lambda l, j: (0, 0)),
            pl.BlockSpec((1, d, d), lambda l, j: (l, 0, j)),
            pl.BlockSpec((1, 1, d), lambda l, j: (l, 0, j)),
        ],
        out_specs=pl.BlockSpec((1, rows, d), lambda l, j: (l, 0, j)),
        out_shape=jax.ShapeDtypeStruct((depth, rows, d6), F32),
        compiler_params=_cparams("arbitrary", "arbitrary"),
        name="ada_modulation",
    )(c_pad, ada_w, ada_b.reshape(depth, 1, d6))
    return out[:, :bsz]


def _group_norm64(z, gmat, w_row):
    ssq = jnp.dot((z * z).astype(BF16), gmat, preferred_element_type=F32)
    return z * lax.rsqrt(ssq * (1.0 / ATTN_HEAD_DIM) + RMS_EPS) * w_row


def _inproj_kernel(x_ref, nw_ref, sc_ref, sh_ref, w_ref, qw_ref, kw_ref, gmat_ref,
                   u_ref, q_ref, k_ref, v_ref, gs_ref, ga_ref):
    x = x_ref[...]
    ms = jnp.mean(x * x, axis=-1, keepdims=True)
    h = x * lax.rsqrt(ms + RMS_EPS) * nw_ref[...]
    h = h * (1.0 + sc_ref[0]) + sh_ref[0]
    hb = h.astype(BF16)
    o = 0
    u_ref[...] = jnp.dot(hb, w_ref[:, o:o + SSM_WIDTH], preferred_element_type=F32).astype(BF16)
    o += SSM_WIDTH
    q = jnp.dot(hb, w_ref[:, o:o + QK_WIDTH], preferred_element_type=F32)
    q_ref[...] = _group_norm64(q, gmat_ref[...], qw_ref[...]).astype(BF16)
    o += QK_WIDTH
    k = jnp.dot(hb, w_ref[:, o:o + QK_WIDTH], preferred_element_type=F32)
    k_ref[...] = _group_norm64(k, gmat_ref[...], kw_ref[...]).astype(BF16)
    o += QK_WIDTH
    v_ref[...] = jnp.dot(hb, w_ref[:, o:o + ATTN_WIDTH], preferred_element_type=F32).astype(BF16)
    o += ATTN_WIDTH
    gs = jnp.dot(hb, w_ref[:, o:o + D_MODEL], preferred_element_type=F32)
    gs_ref[...] = jax.nn.sigmoid(gs).astype(BF16)
    o += D_MODEL
    ga = jnp.dot(hb, w_ref[:, o:o + D_MODEL], preferred_element_type=F32)
    ga_ref[...] = jax.nn.sigmoid(ga).astype(BF16)


def in_projection(x2d, seq, norm_w, sc, sh, w_in, q_norm_w, k_norm_w, tm=512):
    n, d = x2d.shape
    tm = min(tm, seq)
    assert seq % tm == 0 and n % tm == 0
    bsz = n // seq
    width = w_in.shape[1]
    qw = jnp.tile(q_norm_w, QK_WIDTH // ATTN_HEAD_DIM)[None] * (ATTN_HEAD_DIM ** -0.5 * LOG2E)
    kw = jnp.tile(k_norm_w, QK_WIDTH // ATTN_HEAD_DIM)[None]
    grp = np.arange(QK_WIDTH) // ATTN_HEAD_DIM
    gmat = jnp.asarray(grp[:, None] == grp[None, :], BF16)
    row = lambda i: (i, 0)
    fixed = lambda i: (0, 0)
    per_b = lambda i: (i * tm // seq, 0, 0)
    out_w = (SSM_WIDTH, QK_WIDTH, QK_WIDTH, ATTN_WIDTH, D_MODEL, D_MODEL)
    return pl.pallas_call(
        _inproj_kernel,
        grid=(n // tm,),
        in_specs=[
            pl.BlockSpec((tm, d), row),
            pl.BlockSpec((1, d), fixed),
            pl.BlockSpec((1, 1, d), per_b),
            pl.BlockSpec((1, 1, d), per_b),
            pl.BlockSpec((d, width), fixed),
            pl.BlockSpec((1, QK_WIDTH), fixed),
            pl.BlockSpec((1, QK_WIDTH), fixed),
            pl.BlockSpec((QK_WIDTH, QK_WIDTH), fixed),
        ],
        out_specs=[pl.BlockSpec((tm, w), row) for w in out_w],
        out_shape=[jax.ShapeDtypeStruct((n, w), BF16) for w in out_w],
        compiler_params=_cparams("arbitrary"),
        name="in_projection",
    )(x2d, norm_w[None], sc.reshape(bsz, 1, d), sh.reshape(bsz, 1, d), w_in.astype(BF16), qw, kw, gmat)


def _attn_kernel(lam_ref, q_ref, k_ref, v_ref, sw_ref, o_ref, qq_scr, m_scr, acc_scr, *, blk, rows, out_scale):
    qi = pl.program_id(2)
    q = q_ref[...]
    lane = lax.broadcasted_iota(jnp.int32, q.shape, 1)
    zero = jnp.zeros_like(q)
    qq_scr[...] = jnp.concatenate([jnp.where(lane < ATTN_HEAD_DIM, q, zero),
                                   jnp.where(lane >= ATTN_HEAD_DIM, q, zero)], axis=0)
    ones = jnp.ones((blk, LANES), BF16)

    m_scr[...] = jnp.full(m_scr.shape, NEG_BIG, F32)
    acc_scr[...] = jnp.zeros(acc_scr.shape, F32)

    def step(ki, masked):
        off = pl.multiple_of(ki * blk, blk)
        kb = k_ref[pl.ds(off, blk), :]
        vb = jnp.concatenate([v_ref[pl.ds(off, blk), :], ones], axis=1)
        for r0 in range(0, 2 * blk, rows):
            rs = pl.ds(r0, rows)
            s = lax.dot_general(qq_scr[rs, :], kb, (((1,), (1,)), ((), ())), preferred_element_type=F32)
            if masked:
                row = lax.broadcasted_iota(jnp.int32, s.shape, 0) + (r0 % blk)
                col = lax.broadcasted_iota(jnp.int32, s.shape, 1)
                s = jnp.where(col <= row, s, NEG_BIG)
            m = m_scr[rs, :]
            m_new = jnp.maximum(m, jnp.max(s, axis=1, keepdims=True))
            alpha = jnp.exp2(m - m_new)
            p = jnp.exp2(s - m_new).astype(BF16)
            acc_scr[rs, :] = alpha * acc_scr[rs, :] + jnp.dot(p, vb, preferred_element_type=F32)
            m_scr[rs, :] = m_new

    @pl.loop(0, qi)
    def _(ki):
        step(ki, False)

    step(qi, True)
    acc = acc_scr[...]
    o_all = acc[:, :LANES] / acc[:, LANES:]
    o = o_all[:blk] - lam_ref[0] * o_all[blk:]
    ms = jnp.mean(o * o, axis=-1, keepdims=True)
    o_ref[...] = (o * lax.rsqrt(ms + RMS_EPS) * sw_ref[...] * out_scale).astype(BF16)


def diff_attention(q, k, v, seq, lam, subln_w, lambda_init, blk=1024, rows=128):
    n = q.shape[0]
    blk = min(blk, seq)
    rows = min(rows, blk)
    assert seq % blk == 0 and blk % rows == 0
    bsz = n // seq
    nq = seq // blk
    return pl.pallas_call(
        functools.partial(_attn_kernel, blk=blk, rows=rows, out_scale=1.0 - lambda_init),
        grid=(bsz, ATTN_HEADS, nq),
        in_specs=[
            pl.BlockSpec(memory_space=pltpu.SMEM),
            pl.BlockSpec((blk, ATTN_V_DIM), lambda b, h, i: (b * nq + i, h)),
            pl.BlockSpec((seq, ATTN_V_DIM), lambda b, h, i: (b, h)),
            pl.BlockSpec((seq, ATTN_V_DIM), lambda b, h, i: (b, h)),
            pl.BlockSpec((1, ATTN_V_DIM), lambda b, h, i: (0, 0)),
        ],
        out_specs=pl.BlockSpec((blk, ATTN_V_DIM), lambda b, h, i: (b * nq + i, h)),
        out_shape=jax.ShapeDtypeStruct((n, ATTN_WIDTH), BF16),
        scratch_shapes=[pltpu.VMEM((2 * blk, ATTN_V_DIM), BF16), pltpu.VMEM((2 * blk, 1), F32),
                        pltpu.VMEM((2 * blk, 2 * LANES), F32)],
        compiler_params=_cparams("arbitrary", "arbitrary", "arbitrary"),
        name="diff_attention",
    )(lam.reshape(1), q, k, v, subln_w[None])


def s5_matrices(log_dt, a_re, a_im, b_re, b_im, c_re, c_im, d_skip):
    t_len = S5_CHUNK
    g, p, h = b_re.shape
    dt = jnp.exp(log_dt)[:, None]
    mag = jnp.exp(a_re * dt)
    ab_re = mag * jnp.cos(a_im * dt)
    ab_im = mag * jnp.sin(a_im * dt)
    den = a_re * a_re + a_im * a_im
    nr = ab_re - 1.0
    f_re = (nr * a_re + ab_im * a_im) / den
    f_im = (ab_im * a_re - nr * a_im) / den
    bb_re = f_re[..., None] * b_re - f_im[..., None] * b_im
    bb_im = f_re[..., None] * b_im + f_im[..., None] * b_re
    tau = jnp.arange(t_len + 1, dtype=F32)[None, :, None]
    pmag = jnp.exp(a_re[:, None, :] * dt[:, None, :] * tau)
    pw_re = pmag * jnp.cos(a_im[:, None, :] * dt[:, None, :] * tau)
    pw_im = pmag * jnp.sin(a_im[:, None, :] * dt[:, None, :] * tau)
    ca_re = c_re[:, None] * pw_re[:, :, None, :] - c_im[:, None] * pw_im[:, :, None, :]
    ca_im = c_re[:, None] * pw_im[:, :, None, :] + c_im[:, None] * pw_re[:, :, None, :]
    kern = (jnp.einsum("gtop,gpi->gtoi", ca_re[:, :t_len], bb_re)
            - jnp.einsum("gtop,gpi->gtoi", ca_im[:, :t_len], bb_im))
    kern = kern.at[:, 0].add(jax.vmap(jnp.diag)(d_skip))
    lag = np.arange(t_len)[None, :] - np.arange(t_len)[:, None]
    onehot = jnp.asarray(lag[None] == np.arange(t_len)[:, None, None], BF16)
    toep = jnp.einsum("gloi,lst->gisot", kern.astype(BF16), onehot, preferred_element_type=BF16)
    toep = toep.reshape(g, h * t_len, h * t_len)
    rp_re = pw_re[:, t_len - 1::-1][:, :t_len]
    rp_im = pw_im[:, t_len - 1::-1][:, :t_len]
    w_re = rp_re[..., None] * bb_re[:, None] - rp_im[..., None] * bb_im[:, None]
    w_im = rp_re[..., None] * bb_im[:, None] + rp_im[..., None] * bb_re[:, None]
    w_st = jnp.concatenate([w_re, w_im], axis=2).transpose(0, 3, 1, 2).reshape(g, h * t_len, 2 * p)
    v_st = jnp.concatenate([ca_re[:, 1:], -ca_im[:, 1:]], axis=3)
    v_st = v_st.transpose(0, 3, 2, 1).reshape(g, 2 * p, h * t_len)
    at_re, at_im = pw_re[:, t_len], pw_im[:, t_len]
    coef = jnp.stack([jnp.concatenate([at_re, at_re], -1), jnp.concatenate([-at_im, at_im], -1),
                      jnp.concatenate([at_im, -at_im], -1)], axis=1)
    return toep.astype(BF16), w_st.astype(BF16), v_st.astype(BF16), coef


def _s5_kernel(u_ref, toep_ref, w_ref, v_ref, coef_ref, y_ref, s_scr, sw_scr, xp_scr, *, bsz, nchunk):
    hch = u_ref.shape[1]
    xin = jnp.concatenate([u_ref[0, i] for i in range(hch)], axis=1)
    y = jnp.dot(xin, toep_ref[0], preferred_element_type=F32)
    s = jnp.dot(xin, w_ref[0], preferred_element_type=F32)
    half = s.shape[1] // 2
    s_scr[...] = s
    sw_scr[...] = pltpu.roll(s, half, 1)
    coef = coef_ref[0]
    a_cat, b_cat, b_swp = coef[0:1], coef[1:2], coef[2:3]

    def body(c8, carry):
        new = []
        for b in range(bsz):
            x, xs = carry[2 * b], carry[2 * b + 1]
            base = pl.multiple_of(b * nchunk + c8 * SUBLANES, SUBLANES)
            st = s_scr[pl.ds(base, SUBLANES), :]
            sw = sw_scr[pl.ds(base, SUBLANES), :]
            prev = []
            for i in range(SUBLANES):
                prev.append(x)
                x, xs = (a_cat * x + b_cat * xs + st[i:i + 1], a_cat * xs + b_swp * x + sw[i:i + 1])
            xp_scr[pl.ds(base, SUBLANES), :] = jnp.concatenate(prev, axis=0)
            new += [x, xs]
        return tuple(new)

    zero = jnp.zeros((1, s.shape[1]), F32)
    lax.fori_loop(0, nchunk // SUBLANES, body, (zero,) * (2 * bsz))
    xp = xp_scr[...]
    xp_hi = xp.astype(BF16)
    xp_lo = (xp - xp_hi.astype(F32)).astype(BF16)
    y = y + jnp.dot(xp_hi, v_ref[0], preferred_element_type=F32) + jnp.dot(xp_lo, v_ref[0], preferred_element_type=F32)
    y = jax.nn.gelu(y)
    t_len = y.shape[1] // hch
    for i in range(hch):
        y_ref[0, i] = y[:, i * t_len:(i + 1) * t_len].astype(BF16)


def s5_mix(u_t, bsz, seq, toep, w_st, v_st, coef):
    t_len = S5_CHUNK
    n = bsz * seq
    assert seq % (t_len * SUBLANES) == 0
    nchunk = seq // t_len
    rows = n // t_len
    g, hch = SSM_GROUPS, SSM_GROUP_CH
    u4 = u_t.reshape(g, hch, rows, t_len)
    kt = hch * t_len
    p2 = 2 * SSM_STATE
    grp = lambda i: (i, 0, 0)
    y4 = pl.pallas_call(
        functools.partial(_s5_kernel, bsz=bsz, nchunk=nchunk),
        grid=(g,),
        in_specs=[
            pl.BlockSpec((1, hch, rows, t_len), lambda i: (i, 0, 0, 0)),
            pl.BlockSpec((1, kt, kt), grp),
            pl.BlockSpec((1, kt, p2), grp),
            pl.BlockSpec((1, p2, kt), grp),
            pl.BlockSpec((1, 3, p2), grp),
        ],
        out_specs=pl.BlockSpec((1, hch, rows, t_len), lambda i: (i, 0, 0, 0)),
        out_shape=jax.ShapeDtypeStruct((g, hch, rows, t_len), BF16),
        scratch_shapes=[pltpu.VMEM((rows, p2), F32)] * 3,
        compiler_params=_cparams("arbitrary"),
        name="s5_mix",
    )(u4, toep, w_st, v_st, coef)
    return y4.reshape(g * hch, n)


def _mid_kernel(x_ref, yg_ref, oa_ref, gs_ref, ga_ref, g1_ref, sc2_ref, sh2_ref, nw2_ref,
                glu_ref, up_ref, wo_ref, wq_ref, k1_ref, k2_ref,
                x1_ref, h2_ref, s1_ref, s2_ref):
    vg = jnp.dot(yg_ref[...], glu_ref[...], preferred_element_type=F32)
    y_ssm = vg[:, :D_MODEL] * jax.nn.sigmoid(vg[:, D_MODEL:])
    y_att = jnp.dot(oa_ref[...], up_ref[...], preferred_element_type=F32)
    merged = gs_ref[...].astype(F32) * y_ssm + ga_ref[...].astype(F32) * y_att
    x1 = x_ref[...] + g1_ref[0] * jnp.dot(merged.astype(BF16), wo_ref[...], preferred_element_type=F32)
    x1_ref[...] = x1
    ms = jnp.mean(x1 * x1, axis=-1, keepdims=True)
    h2 = x1 * lax.rsqrt(ms + RMS_EPS) * nw2_ref[...]
    h2 = h2 * (1.0 + sc2_ref[0]) + sh2_ref[0]
    h2_ref[...] = h2
    qp = jnp.dot(h2.astype(BF16), wq_ref[...], preferred_element_type=F32).astype(BF16)
    nt = (((1,), (1,)), ((), ()))
    for hh in range(PEER_HEADS):
        o = hh * 2 * PEER_HALF
        s1_ref[hh] = lax.dot_general(k1_ref[...], qp[:, o:o + PEER_HALF], nt, preferred_element_type=F32)
        s2_ref[hh] = lax.dot_general(k2_ref[...], qp[:, o + PEER_HALF:o + 2 * PEER_HALF], nt,
                                     preferred_element_type=F32)


def mid_block(x2d, seq, yg, o_att, gs, ga, g1, sc2, sh2, norm2_w, glu_w, up_w, w_out, peer_wq, k1, k2, tm=256):
    n, d = x2d.shape
    tm = min(tm, seq)
    assert seq % tm == 0
    bsz = n // seq
    row = lambda i: (i, 0)
    fixed = lambda i: (0, 0)
    per_b = lambda i: (i * tm // seq, 0, 0)
    full = lambda a: pl.BlockSpec(a.shape, fixed)
    weights = [w.astype(BF16) for w in (glu_w, up_w, w_out, peer_wq, k1, k2)]
    sc_shape = (PEER_HEADS, PEER_N_KEYS, n)
    return pl.pallas_call(
        _mid_kernel,
        grid=(n // tm,),
        in_specs=[
            pl.BlockSpec((tm, d), row),
            pl.BlockSpec((tm, SSM_WIDTH), row),
            pl.BlockSpec((tm, ATTN_WIDTH), row),
            pl.BlockSpec((tm, d), row),
            pl.BlockSpec((tm, d), row),
            pl.BlockSpec((1, 1, d), per_b),
            pl.BlockSpec((1, 1, d), per_b),
            pl.BlockSpec((1, 1, d), per_b),
            pl.BlockSpec((1, d), fixed),
        ] + [full(w) for w in weights],
        out_specs=[
            pl.BlockSpec((tm, d), row),
            pl.BlockSpec((tm, d), row),
            pl.BlockSpec((PEER_HEADS, PEER_N_KEYS, tm), lambda i: (0, 0, i)),
            pl.BlockSpec((PEER_HEADS, PEER_N_KEYS, tm), lambda i: (0, 0, i)),
        ],
        out_shape=[jax.ShapeDtypeStruct((n, d), F32), jax.ShapeDtypeStruct((n, d), F32),
                   jax.ShapeDtypeStruct(sc_shape, F32), jax.ShapeDtypeStruct(sc_shape, F32)],
        compiler_params=_cparams("arbitrary"),
        name="mid_block",
    )(x2d, yg, o_att, gs, ga, g1.reshape(bsz, 1, d), sc2.reshape(bsz, 1, d), sh2.reshape(bsz, 1, d),
      norm2_w[None], *weights)


def _top16_rows(s):
    nrows = float(s.shape[0])
    rowid = lax.broadcasted_iota(jnp.int32, s.shape, 0).astype(F32)
    vals, ids = [], []
    for _ in range(PEER_TOPK):
        m = jnp.max(s, axis=0, keepdims=True)
        i = jnp.min(jnp.where(s == m, rowid, nrows), axis=0, keepdims=True)
        vals.append(m)
        ids.append(i)
        s = jnp.where(rowid == i, -jnp.inf, s)
    return jnp.concatenate(vals, axis=0), jnp.concatenate(ids, axis=0)


def _topk_kernel(s1_ref, s2_ref, g_ref, e_ref, *, width):
    def tile(ti, _):
        cols = pl.ds(pl.multiple_of(ti * width, width), width)
        gate, expert = _topk_tile(s1_ref[0, :, cols], s2_ref[0, :, cols])
        g_ref[0, :, cols] = gate
        e_ref[0, :, cols] = expert
        return 0

    lax.fori_loop(0, s1_ref.shape[2] // width, tile, 0)


def _topk_tile(s1, s2):
    v1, i1 = _top16_rows(s1)
    v2, i2 = _top16_rows(s2)
    half = PEER_TOPK // 2
    sub = lax.broadcasted_iota(jnp.int32, (half, v1.shape[1]), 0).astype(F32)
    cand, pay, flat = [], [], []
    for i in range(half):
        cand.append(v1[i:i + 1] + v2[:half])
        pay.append(i1[i:i + 1] * PEER_N_KEYS + i2[:half])
        flat.append(sub + i * PEER_TOPK)
    cand.append(v1[0:1] + v2[half:])
    pay.append(i1[0:1] * PEER_N_KEYS + i2[half:])
    flat.append(sub + half)
    cand.append(v1[half:] + v2[0:1])
    pay.append(i1[half:] * PEER_N_KEYS + i2[0:1])
    flat.append((sub + half) * PEER_TOPK)
    cand = jnp.concatenate(cand, axis=0)
    pay = jnp.concatenate(pay, axis=0)
    flat = jnp.concatenate(flat, axis=0)
    big = float(PEER_TOPK * PEER_TOPK)
    sc, ex = [], []
    for _ in range(PEER_TOPK):
        m = jnp.max(cand, axis=0, keepdims=True)
        f = jnp.min(jnp.where(cand == m, flat, big), axis=0, keepdims=True)
        sel = flat == f
        ex.append(jnp.max(jnp.where(sel, pay, -1.0), axis=0, keepdims=True))
        sc.append(m)
        cand = jnp.where(sel, -jnp.inf, cand)
    sc = jnp.concatenate(sc, axis=0)
    p = jnp.exp(sc - sc[0:1])
    return p / jnp.sum(p, axis=0, keepdims=True), jnp.concatenate(ex, axis=0).astype(jnp.int32)


def peer_topk(s1, s2, tn=2048, width=2 * LANES):
    heads, nkeys, n = s1.shape
    tn = min(tn, n)
    width = min(width, tn)
    assert n % tn == 0 and tn % width == 0
    blk_in = pl.BlockSpec((1, nkeys, tn), lambda h, i: (h, 0, i))
    blk_out = pl.BlockSpec((1, PEER_TOPK, tn), lambda h, i: (h, 0, i))
    return pl.pallas_call(
        functools.partial(_topk_kernel, width=width),
        grid=(heads, n // tn),
        in_specs=[blk_in, blk_in],
        out_specs=[blk_out, blk_out],
        out_shape=[jax.ShapeDtypeStruct((heads, PEER_TOPK, n), F32),
                   jax.ShapeDtypeStruct((heads, PEER_TOPK, n), jnp.int32)],
        compiler_params=_cparams("arbitrary", "arbitrary"),
        name="peer_topk",
    )(s1, s2)


PACK_ROWS = 2 * SUBLANES
EXPANDED = PEER_SLOTS * PACK_ROWS


def pack_expert_pairs(tab):
    n_exp, d = tab.shape
    half = n_exp // 2
    bits = lax.bitcast_convert_type(tab.astype(BF16), jnp.uint16).astype(jnp.uint32)
    words = (bits[:half] << 16) | bits[half:]
    return words.reshape(half * (d // LANES), LANES)


def _expand_matrix():
    k = np.arange(EXPANDED)
    return jnp.asarray((k[None, :] // PACK_ROWS) == np.arange(PEER_SLOTS)[:, None], BF16)


def _split2(x):
    hi = x.astype(BF16)
    lo = (x - hi.astype(F32)).astype(BF16)
    return hi, lo


def _gathered(tab_ref, row_ref, t):
    tok = row_ref.at[pl.ds(pl.multiple_of(t * PEER_SLOTS, PEER_SLOTS), PEER_SLOTS)]
    tiles = []
    for j in range(PEER_SLOTS):
        words = tab_ref[pl.ds(pl.multiple_of(tok[j], SUBLANES), SUBLANES), :]
        tiles.append(pltpu.bitcast(words, BF16))
    return jnp.concatenate(tiles, axis=0)


def _row_target(half_b):
    sub = lax.broadcasted_iota(jnp.int32, half_b.shape, 0).astype(F32)
    return half_b + 2.0 * sub


def _peer_u_kernel(row_ref, halfx_ref, h_ref, col_ref, tab_ref, r_ref, *, tb):
    kmod = (lax.broadcasted_iota(jnp.int32, (SUBLANES, EXPANDED), 1) % PACK_ROWS).astype(F32)
    tok_row = lax.broadcasted_iota(jnp.int32, (SUBLANES, EXPANDED), 0)
    nt = (((1,), (1,)), ((), ()))

    def group(gi, _):
        base = pl.multiple_of(gi * SUBLANES, SUBLANES)

        def token(i, acc):
            t = base + i
            hh = jnp.concatenate(_split2(h_ref[t]), axis=0)
            w = _gathered(tab_ref, row_ref, t)
            s = lax.dot_general(hh, w, nt, preferred_element_type=F32)
            both = s[:SUBLANES] + s[SUBLANES:]
            half_b = jnp.broadcast_to(halfx_ref[pl.ds(t, 1), :], (SUBLANES, EXPANDED))
            picked = jnp.where(kmod == _row_target(half_b), both, 0.0)
            red = jnp.broadcast_to(jnp.sum(picked, axis=0, keepdims=True), acc.shape)
            return jnp.where(tok_row == i, red, acc)

        acc = lax.fori_loop(0, SUBLANES, token, jnp.zeros((SUBLANES, EXPANDED), F32), unroll=4)
        tot = jnp.dot(jnp.concatenate(_split2(acc), axis=0), col_ref[...], preferred_element_type=F32)
        r_ref[pl.ds(base, SUBLANES), :] = tot[:SUBLANES] + tot[SUBLANES:]
        return 0

    lax.fori_loop(0, tb // SUBLANES, group, 0)


def _peer_v_kernel(row_ref, half_ref, r_ref, g_ref, x_ref, g2_ref, exp_ref, tab_ref, o_ref, x_scr, *, tb):
    g2 = g2_ref[0]
    kmod = (lax.broadcasted_iota(jnp.int32, (SUBLANES, EXPANDED), 1) % PACK_ROWS).astype(F32)

    def group(gi, _):
        base = pl.multiple_of(gi * SUBLANES, SUBLANES)
        coef = g_ref[pl.ds(base, SUBLANES), :] * jax.nn.gelu(r_ref[pl.ds(base, SUBLANES), :])
        c_hi, c_lo = _split2(coef)
        lhs = jnp.concatenate([c_hi, c_lo, half_ref[pl.ds(base, SUBLANES), :].astype(BF16)], axis=0)
        x_scr[...] = jnp.dot(lhs, exp_ref[...], preferred_element_type=F32)

        def token(i, _):
            t = base + i
            bc = lambda r: jnp.broadcast_to(x_scr[pl.ds(r, 1), :], (SUBLANES, EXPANDED))
            sel = kmod == _row_target(bc(2 * SUBLANES + i))
            l_hi = jnp.where(sel, bc(i), 0.0)
            l_lo = jnp.where(sel, bc(SUBLANES + i), 0.0)
            lmat = jnp.concatenate([l_hi, l_lo], axis=0).astype(BF16)
            out = jnp.dot(lmat, _gathered(tab_ref, row_ref, t), preferred_element_type=F32)
            o_ref[t] = x_ref[t] + g2 * (out[:SUBLANES] + out[SUBLANES:])
            return 0

        lax.fori_loop(0, SUBLANES, token, 0, unroll=8)
        return 0

    lax.fori_loop(0, tb // SUBLANES, group, 0)


def peer_u_dots(rows, half, h2, u_words, seq, tb=256):
    n = half.shape[0]
    tb = min(tb, seq)
    assert seq % tb == 0
    d_sub = D_MODEL // LANES
    half_x = jnp.repeat(half, PACK_ROWS, axis=1)
    return pl.pallas_call(
        functools.partial(_peer_u_kernel, tb=tb),
        grid=(n // tb,),
        in_specs=[
            pl.BlockSpec((tb * PEER_SLOTS,), lambda i: (i,), memory_space=pltpu.SMEM),
            pl.BlockSpec((tb, EXPANDED), lambda i: (i, 0)),
            pl.BlockSpec((tb, d_sub, LANES), lambda i: (i, 0, 0)),
            pl.BlockSpec((EXPANDED, PEER_SLOTS), lambda i: (0, 0)),
            pl.BlockSpec(memory_space=pltpu.VMEM),
        ],
        out_specs=pl.BlockSpec((tb, PEER_SLOTS), lambda i: (i, 0)),
        out_shape=jax.ShapeDtypeStruct((n, PEER_SLOTS), F32),
        compiler_params=_cparams("arbitrary"),
        name="peer_u_dots",
    )(rows, half_x, h2.reshape(n, d_sub, LANES), _expand_matrix().T, u_words)


def peer_v_mix(rows, half, r, gate, x1, g2, v_words, seq, tb=256):
    n = half.shape[0]
    tb = min(tb, seq)
    assert seq % tb == 0
    bsz = n // seq
    d_sub = D_MODEL // LANES
    tok = lambda i: (i, 0)
    tok3 = lambda i: (i, 0, 0)
    out = pl.pallas_call(
        functools.partial(_peer_v_kernel, tb=tb),
        grid=(n // tb,),
        in_specs=[
            pl.BlockSpec((tb * PEER_SLOTS,), lambda i: (i,), memory_space=pltpu.SMEM),
            pl.BlockSpec((tb, PEER_SLOTS), tok),
            pl.BlockSpec((tb, PEER_SLOTS), tok),
            pl.BlockSpec((tb, PEER_SLOTS), tok),
            pl.BlockSpec((tb, d_sub, LANES), tok3),
            pl.BlockSpec((1, d_sub, LANES), lambda i: (i * tb // seq, 0, 0)),
            pl.BlockSpec((PEER_SLOTS, EXPANDED), lambda i: (0, 0)),
            pl.BlockSpec(memory_space=pltpu.VMEM),
        ],
        out_specs=pl.BlockSpec((tb, d_sub, LANES), tok3),
        out_shape=jax.ShapeDtypeStruct((n, d_sub, LANES), F32),
        scratch_shapes=[pltpu.VMEM((3 * SUBLANES, EXPANDED), F32)],
        compiler_params=_cparams("arbitrary"),
        name="peer_v_mix",
    )(rows, half, r, gate, x1.reshape(n, d_sub, LANES), g2.reshape(bsz, d_sub, LANES), _expand_matrix(), v_words)
    return out.reshape(n, D_MODEL)


def peer_block(x1, h2, s1, s2, g2, u_tab, v_tab, seq):
    n = x1.shape[0]
    gate, expert = peer_topk(s1, s2)
    slots = lambda a: a.transpose(2, 0, 1).reshape(n, PEER_SLOTS)
    idx = slots(expert)
    rows = ((idx & (PEER_HALF_EXPERTS - 1)) * SUBLANES).reshape(-1)
    half = (1 - idx // PEER_HALF_EXPERTS).astype(F32)
    r = peer_u_dots(rows, half, h2, pack_expert_pairs(u_tab), seq)
    return peer_v_mix(rows, half, r, slots(gate), x1, g2, pack_expert_pairs(v_tab), seq)


def kernel(x, c, ada_w, ada_b, norm1_w, norm2_w, w_in, ssm_log_dt, ssm_a_re, ssm_a_im, ssm_b_re, ssm_b_im, ssm_c_re, ssm_c_im, ssm_d, ssm_glu_w, q_norm_w, k_norm_w, lambda_q1, lambda_k1, lambda_q2, lambda_k2, subln_w, attn_up_w, w_out, peer_wq, peer_k1, peer_k2, peer_u, peer_v):
    bsz, seq, d = x.shape
    n = bsz * seq
    depth = ada_w.shape[0]
    mod = ada_modulation(c, ada_w, ada_b)
    x2d = x.reshape(n, d)
    for l in range(depth):
        sh1, sc1, g1, sh2, sc2, g2 = jnp.split(mod[l], 6, axis=-1)
        u, q, k, v, gs, ga = in_projection(x2d, seq, norm1_w[l], sc1, sh1, w_in[l], q_norm_w[l], k_norm_w[l])
        mats = s5_matrices(ssm_log_dt[l], ssm_a_re[l], ssm_a_im[l], ssm_b_re[l], ssm_b_im[l],
                           ssm_c_re[l], ssm_c_im[l], ssm_d[l])
        yg = s5_mix(u.T, bsz, seq, *mats).T
        lambda_init = 0.8 - 0.6 * math.exp(-0.3 * l)
        lam = (jnp.exp(jnp.sum(lambda_q1[l] * lambda_k1[l])) - jnp.exp(jnp.sum(lambda_q2[l] * lambda_k2[l]))
               + lambda_init)
        o_att = diff_attention(q, k, v, seq, lam, subln_w[l], lambda_init)
        x1, h2, s1, s2 = mid_block(x2d, seq, yg, o_att, gs, ga, g1, sc2, sh2, norm2_w[l], ssm_glu_w[l],
                                   attn_up_w[l], w_out[l], peer_wq[l], peer_k1[l], peer_k2[l])
        x2d = peer_block(x1, h2, s1, s2, g2, peer_u[l], peer_v[l], seq)
    return x2d.reshape(bsz, seq, d)
```

```python
import functools
import math

import jax
import jax.numpy as jnp
import numpy as np
from jax import lax
from jax.experimental import pallas as pl
from jax.experimental.pallas import tpu as pltpu

F32 = jnp.float32
BF16 = jnp.bfloat16

D_MODEL = 1024
SSM_GROUPS = 32
SSM_GROUP_CH = 16
SSM_WIDTH = SSM_GROUPS * SSM_GROUP_CH
SSM_STATE = 64
ATTN_HEADS = 4
ATTN_HEAD_DIM = 64
ATTN_V_DIM = 2 * ATTN_HEAD_DIM
QK_WIDTH = ATTN_HEADS * 2 * ATTN_HEAD_DIM
ATTN_WIDTH = ATTN_HEADS * ATTN_V_DIM
PEER_HEADS = 8
PEER_N_KEYS = 128
PEER_HALF = 128
PEER_TOPK = 16
PEER_SLOTS = PEER_HEADS * PEER_TOPK
PEER_HALF_EXPERTS = PEER_N_KEYS * PEER_N_KEYS // 2
RMS_EPS = 1e-6
NEG_BIG = -1e30
LOG2E = 1.4426950408889634

LANES = 128
SUBLANES = 8
S5_CHUNK = LANES
VMEM_LIMIT = 56 * 1024 * 1024


def _cparams(*sem):
    return pltpu.CompilerParams(dimension_semantics=sem, vmem_limit_bytes=VMEM_LIMIT)


def _ada_kernel(c_ref, w_ref, b_ref, o_ref):
    c = c_ref[...]
    ca = c * jax.nn.sigmoid(c)
    o_ref[0] = jnp.dot(ca, w_ref[0], preferred_element_type=F32) + b_ref[0]


def ada_modulation(c, ada_w, ada_b):
    depth, d, d6 = ada_w.shape
    bsz = c.shape[0]
    rows = -(-bsz // SUBLANES) * SUBLANES
    c_pad = jnp.zeros((rows, d), F32).at[:bsz].set(c)
    out = pl.pallas_call(
        _ada_kernel,
        grid=(depth, d6 // d),
        in_specs=[
            pl.BlockSpec((rows, d), lambda l, j: (0, 0)),
            pl.BlockSpec((1, d, d), lambda l, j: (l, 0, j)),
            pl.BlockSpec((1, 1, d), lambda l, j: (l, 0, j)),
        ],
        out_specs=pl.BlockSpec((1, rows, d), lambda l, j: (l, 0, j)),
        out_shape=jax.ShapeDtypeStruct((depth, rows, d6), F32),
        compiler_params=_cparams("arbitrary", "arbitrary"),
        name="ada_modulation",
    )(c_pad, ada_w, ada_b.reshape(depth, 1, d6))
    return out[:, :bsz]


def _group_norm64(z, gmat, w_row):
    ssq = jnp.dot((z * z).astype(BF16), gmat, preferred_element_type=F32)
    return z * lax.rsqrt(ssq * (1.0 / ATTN_HEAD_DIM) + RMS_EPS) * w_row


def _inproj_kernel(x_ref, nw_ref, sc_ref, sh_ref, w_ref, wut_ref, qw_ref, kw_ref, gmat_ref,
                   u_ref, q_ref, k_ref, v_ref, gs_ref, ga_ref):
    x = x_ref[...]
    ms = jnp.mean(x * x, axis=-1, keepdims=True)
    h = x * lax.rsqrt(ms + RMS_EPS) * nw_ref[...]
    h = h * (1.0 + sc_ref[0]) + sh_ref[0]
    hb = h.astype(BF16)
    o = 0
    u_ref[...] = lax.dot_general(wut_ref[...], hb, (((1,), (1,)), ((), ())),
                                 preferred_element_type=F32).astype(BF16)
    o += SSM_WIDTH
    q = jnp.dot(hb, w_ref[:, o:o + QK_WIDTH], preferred_element_type=F32)
    q_ref[...] = _group_norm64(q, gmat_ref[...], qw_ref[...]).astype(BF16)
    o += QK_WIDTH
    k = jnp.dot(hb, w_ref[:, o:o + QK_WIDTH], preferred_element_type=F32)
    k_ref[...] = _group_norm64(k, gmat_ref[...], kw_ref[...]).astype(BF16)
    o += QK_WIDTH
    v_ref[...] = jnp.dot(hb, w_ref[:, o:o + ATTN_WIDTH], preferred_element_type=F32).astype(BF16)
    o += ATTN_WIDTH
    gs = jnp.dot(hb, w_ref[:, o:o + D_MODEL], preferred_element_type=F32)
    gs_ref[...] = jax.nn.sigmoid(gs).astype(BF16)
    o += D_MODEL
    ga = jnp.dot(hb, w_ref[:, o:o + D_MODEL], preferred_element_type=F32)
    ga_ref[...] = jax.nn.sigmoid(ga).astype(BF16)


def in_projection(x2d, seq, norm_w, sc, sh, w_in, q_norm_w, k_norm_w, tm=512):
    n, d = x2d.shape
    tm = min(tm, seq)
    assert seq % tm == 0 and n % tm == 0
    bsz = n // seq
    width = w_in.shape[1]
    qw = jnp.tile(q_norm_w, QK_WIDTH // ATTN_HEAD_DIM)[None] * (ATTN_HEAD_DIM ** -0.5 * LOG2E)
    kw = jnp.tile(k_norm_w, QK_WIDTH // ATTN_HEAD_DIM)[None]
    grp = np.arange(QK_WIDTH) // ATTN_HEAD_DIM
    gmat = jnp.asarray(grp[:, None] == grp[None, :], BF16)
    row = lambda i: (i, 0)
    fixed = lambda i: (0, 0)
    per_b = lambda i: (i * tm // seq, 0, 0)
    out_w = (SSM_WIDTH, QK_WIDTH, QK_WIDTH, ATTN_WIDTH, D_MODEL, D_MODEL)
    return pl.pallas_call(
        _inproj_kernel,
        grid=(n // tm,),
        in_specs=[
            pl.BlockSpec((tm, d), row),
            pl.BlockSpec((1, d), fixed),
            pl.BlockSpec((1, 1, d), per_b),
            pl.BlockSpec((1, 1, d), per_b),
            pl.BlockSpec((d, width), fixed),
            pl.BlockSpec((SSM_WIDTH, d), fixed),
            pl.BlockSpec((1, QK_WIDTH), fixed),
            pl.BlockSpec((1, QK_WIDTH), fixed),
            pl.BlockSpec((QK_WIDTH, QK_WIDTH), fixed),
        ],
        out_specs=[pl.BlockSpec((SSM_WIDTH, tm), lambda i: (0, i))] + [pl.BlockSpec((tm, w), row) for w in out_w[1:]],
        out_shape=[jax.ShapeDtypeStruct((SSM_WIDTH, n), BF16)] + [jax.ShapeDtypeStruct((n, w), BF16) for w in out_w[1:]],
        compiler_params=_cparams("arbitrary"),
        name="in_projection",
    )(x2d, norm_w[None], sc.reshape(bsz, 1, d), sh.reshape(bsz, 1, d), w_in.astype(BF16),
      w_in[:, :SSM_WIDTH].T.astype(BF16), qw, kw, gmat)


def _attn_kernel(lam_ref, q_ref, k_ref, v_ref, sw_ref, o_ref, qq_scr, m_scr, acc_scr, *, blk, rows, out_scale):
    qi = pl.program_id(2)
    q = q_ref[...]
    lane = lax.broadcasted_iota(jnp.int32, q.shape, 1)
    zero = jnp.zeros_like(q)
    qq_scr[...] = jnp.concatenate([jnp.where(lane < ATTN_HEAD_DIM, q, zero),
                                   jnp.where(lane >= ATTN_HEAD_DIM, q, zero)], axis=0)
    ones = jnp.ones((blk, LANES), BF16)

    m_scr[...] = jnp.full(m_scr.shape, NEG_BIG, F32)
    acc_scr[...] = jnp.zeros(acc_scr.shape, F32)

    def step(ki, masked):
        off = pl.multiple_of(ki * blk, blk)
        kb = k_ref[pl.ds(off, blk), :]
        vb = jnp.concatenate([v_ref[pl.ds(off, blk), :], ones], axis=1)
        for r0 in range(0, 2 * blk, rows):
            rs = pl.ds(r0, rows)
            s = lax.dot_general(qq_scr[rs, :], kb, (((1,), (1,)), ((), ())), preferred_element_type=F32)
            if masked:
                row = lax.broadcasted_iota(jnp.int32, s.shape, 0) + (r0 % blk)
                col = lax.broadcasted_iota(jnp.int32, s.shape, 1)
                s = jnp.where(col <= row, s, NEG_BIG)
            m = m_scr[rs, :]
            m_new = jnp.maximum(m, jnp.max(s, axis=1, keepdims=True))
            alpha = jnp.exp2(m - m_new)
            p = jnp.exp2(s - m_new).astype(BF16)
            acc_scr[rs, :] = alpha * acc_scr[rs, :] + jnp.dot(p, vb, preferred_element_type=F32)
            m_scr[rs, :] = m_new

    @pl.loop(0, qi)
    def _(ki):
        step(ki, False)

    step(qi, True)
    acc = acc_scr[...]
    o_all = acc[:, :LANES] / acc[:, LANES:]
    o = o_all[:blk] - lam_ref[0] * o_all[blk:]
    ms = jnp.mean(o * o, axis=-1, keepdims=True)
    o_ref[...] = (o * lax.rsqrt(ms + RMS_EPS) * sw_ref[...] * out_scale).astype(BF16)


def diff_attention(q, k, v, seq, lam, subln_w, lambda_init, blk=1024, rows=128):
    n = q.shape[0]
    blk = min(blk, seq)
    rows = min(rows, blk)
    assert seq % blk == 0 and blk % rows == 0
    bsz = n // seq
    nq = seq // blk
    return pl.pallas_call(
        functools.partial(_attn_kernel, blk=blk, rows=rows, out_scale=1.0 - lambda_init),
        grid=(bsz, ATTN_HEADS, nq),
        in_specs=[
            pl.BlockSpec(memory_space=pltpu.SMEM),
            pl.BlockSpec((blk, ATTN_V_DIM), lambda b, h, i: (b * nq + i, h)),
            pl.BlockSpec((seq, ATTN_V_DIM), lambda b, h, i: (b, h)),
            pl.BlockSpec((seq, ATTN_V_DIM), lambda b, h, i: (b, h)),
            pl.BlockSpec((1, ATTN_V_DIM), lambda b, h, i: (0, 0)),
        ],
        out_specs=pl.BlockSpec((blk, ATTN_V_DIM), lambda b, h, i: (b * nq + i, h)),
        out_shape=jax.ShapeDtypeStruct((n, ATTN_WIDTH), BF16),
        scratch_shapes=[pltpu.VMEM((2 * blk, ATTN_V_DIM), BF16), pltpu.VMEM((2 * blk, 1), F32),
                        pltpu.VMEM((2 * blk, 2 * LANES), F32)],
        compiler_params=_cparams("arbitrary", "arbitrary", "arbitrary"),
        name="diff_attention",
    )(lam.reshape(1), q, k, v, subln_w[None])


def s5_matrices(log_dt, a_re, a_im, b_re, b_im, c_re, c_im, d_skip):
    t_len = S5_CHUNK
    g, p, h = b_re.shape
    dt = jnp.exp(log_dt)[:, None]
    mag = jnp.exp(a_re * dt)
    ab_re = mag * jnp.cos(a_im * dt)
    ab_im = mag * jnp.sin(a_im * dt)
    den = a_re * a_re + a_im * a_im
    nr = ab_re - 1.0
    f_re = (nr * a_re + ab_im * a_im) / den
    f_im = (ab_im * a_re - nr * a_im) / den
    bb_re = f_re[..., None] * b_re - f_im[..., None] * b_im
    bb_im = f_re[..., None] * b_im + f_im[..., None] * b_re
    tau = jnp.arange(t_len + 1, dtype=F32)[None, :, None]
    pmag = jnp.exp(a_re[:, None, :] * dt[:, None, :] * tau)
    pw_re = pmag * jnp.cos(a_im[:, None, :] * dt[:, None, :] * tau)
    pw_im = pmag * jnp.sin(a_im[:, None, :] * dt[:, None, :] * tau)
    ca_re = c_re[:, None] * pw_re[:, :, None, :] - c_im[:, None] * pw_im[:, :, None, :]
    ca_im = c_re[:, None] * pw_im[:, :, None, :] + c_im[:, None] * pw_re[:, :, None, :]
    kern = (jnp.einsum("gtop,gpi->gtoi", ca_re[:, :t_len], bb_re)
            - jnp.einsum("gtop,gpi->gtoi", ca_im[:, :t_len], bb_im))
    kern = kern.at[:, 0].add(jax.vmap(jnp.diag)(d_skip))
    lag = np.arange(t_len)[None, :] - np.arange(t_len)[:, None]
    onehot = jnp.asarray(lag[None] == np.arange(t_len)[:, None, None], BF16)
    toep = jnp.einsum("gloi,lst->gisot", kern.astype(BF16), onehot, preferred_element_type=BF16)
    toep = toep.reshape(g, h * t_len, h * t_len)
    rp_re = pw_re[:, t_len - 1::-1][:, :t_len]
    rp_im = pw_im[:, t_len - 1::-1][:, :t_len]
    w_re = rp_re[..., None] * bb_re[:, None] - rp_im[..., None] * bb_im[:, None]
    w_im = rp_re[..., None] * bb_im[:, None] + rp_im[..., None] * bb_re[:, None]
    w_st = jnp.concatenate([w_re, w_im], axis=2).transpose(0, 3, 1, 2).reshape(g, h * t_len, 2 * p)
    v_st = jnp.concatenate([ca_re[:, 1:], -ca_im[:, 1:]], axis=3)
    v_st = v_st.transpose(0, 3, 2, 1).reshape(g, 2 * p, h * t_len)
    at_re, at_im = pw_re[:, t_len], pw_im[:, t_len]
    coef = jnp.stack([jnp.concatenate([at_re, at_re], -1), jnp.concatenate([-at_im, at_im], -1),
                      jnp.concatenate([at_im, -at_im], -1)], axis=1)
    return toep.astype(BF16), w_st.astype(BF16), v_st.astype(BF16), coef


def _s5_kernel(u_ref, toep_ref, w_ref, v_ref, coef_ref, y_ref, s_scr, sw_scr, xp_scr, *, bsz, nchunk):
    hch = u_ref.shape[1]
    xin = jnp.concatenate([u_ref[0, i] for i in range(hch)], axis=1)
    y = jnp.dot(xin, toep_ref[0], preferred_element_type=F32)
    s = jnp.dot(xin, w_ref[0], preferred_element_type=F32)
    half = s.shape[1] // 2
    s_scr[...] = s
    sw_scr[...] = pltpu.roll(s, half, 1)
    coef = coef_ref[0]
    a_cat, b_cat, b_swp = coef[0:1], coef[1:2], coef[2:3]

    def body(c8, carry):
        new = []
        for b in range(bsz):
            x, xs = carry[2 * b], carry[2 * b + 1]
            base = pl.multiple_of(b * nchunk + c8 * SUBLANES, SUBLANES)
            st = s_scr[pl.ds(base, SUBLANES), :]
            sw = sw_scr[pl.ds(base, SUBLANES), :]
            prev = []
            for i in range(SUBLANES):
                prev.append(x)
                x, xs = (a_cat * x + b_cat * xs + st[i:i + 1], a_cat * xs + b_swp * x + sw[i:i + 1])
            xp_scr[pl.ds(base, SUBLANES), :] = jnp.concatenate(prev, axis=0)
            new += [x, xs]
        return tuple(new)

    zero = jnp.zeros((1, s.shape[1]), F32)
    lax.fori_loop(0, nchunk // SUBLANES, body, (zero,) * (2 * bsz))
    xp = xp_scr[...]
    xp_hi = xp.astype(BF16)
    xp_lo = (xp - xp_hi.astype(F32)).astype(BF16)
    y = y + jnp.dot(xp_hi, v_ref[0], preferred_element_type=F32) + jnp.dot(xp_lo, v_ref[0], preferred_element_type=F32)
    y = jax.nn.gelu(y)
    t_len = y.shape[1] // hch
    for i in range(hch):
        y_ref[0, i] = y[:, i * t_len:(i + 1) * t_len].astype(BF16)


def s5_mix(u_t, bsz, seq, toep, w_st, v_st, coef):
    t_len = S5_CHUNK
    n = bsz * seq
    assert seq % (t_len * SUBLANES) == 0
    nchunk = seq // t_len
    rows = n // t_len
    g, hch = SSM_GROUPS, SSM_GROUP_CH
    u4 = u_t.reshape(g, hch, rows, t_len)
    kt = hch * t_len
    p2 = 2 * SSM_STATE
    grp = lambda i: (i, 0, 0)
    y4 = pl.pallas_call(
        functools.partial(_s5_kernel, bsz=bsz, nchunk=nchunk),
        grid=(g,),
        in_specs=[
            pl.BlockSpec((1, hch, rows, t_len), lambda i: (i, 0, 0, 0)),
            pl.BlockSpec((1, kt, kt), grp),
            pl.BlockSpec((1, kt, p2), grp),
            pl.BlockSpec((1, p2, kt), grp),
            pl.BlockSpec((1, 3, p2), grp),
        ],
        out_specs=pl.BlockSpec((1, hch, rows, t_len), lambda i: (i, 0, 0, 0)),
        out_shape=jax.ShapeDtypeStruct((g, hch, rows, t_len), BF16),
        scratch_shapes=[pltpu.VMEM((rows, p2), F32)] * 3,
        compiler_params=_cparams("arbitrary"),
        name="s5_mix",
    )(u4, toep, w_st, v_st, coef)
    return y4.reshape(g * hch, n)


def _mid_kernel(x_ref, yg_ref, oa_ref, gs_ref, ga_ref, g1_ref, sc2_ref, sh2_ref, nw2_ref,
                glu_ref, up_ref, wo_ref, wq_ref, k1_ref, k2_ref,
                x1_ref, h2_ref, s1_ref, s2_ref):
    vg = lax.dot_general(yg_ref[...], glu_ref[...], (((0,), (0,)), ((), ())), preferred_element_type=F32)
    y_ssm = vg[:, :D_MODEL] * jax.nn.sigmoid(vg[:, D_MODEL:])
    y_att = jnp.dot(oa_ref[...], up_ref[...], preferred_element_type=F32)
    merged = gs_ref[...].astype(F32) * y_ssm + ga_ref[...].astype(F32) * y_att
    x1 = x_ref[...] + g1_ref[0] * jnp.dot(merged.astype(BF16), wo_ref[...], preferred_element_type=F32)
    x1_ref[...] = x1
    ms = jnp.mean(x1 * x1, axis=-1, keepdims=True)
    h2 = x1 * lax.rsqrt(ms + RMS_EPS) * nw2_ref[...]
    h2 = h2 * (1.0 + sc2_ref[0]) + sh2_ref[0]
    h2_ref[...] = h2
    qp = jnp.dot(h2.astype(BF16), wq_ref[...], preferred_element_type=F32).astype(BF16)
    nt = (((1,), (1,)), ((), ()))
    for hh in range(PEER_HEADS):
        o = hh * 2 * PEER_HALF
        s1_ref[hh] = lax.dot_general(k1_ref[...], qp[:, o:o + PEER_HALF], nt, preferred_element_type=F32)
        s2_ref[hh] = lax.dot_general(k2_ref[...], qp[:, o + PEER_HALF:o + 2 * PEER_HALF], nt,
                                     preferred_element_type=F32)


def mid_block(x2d, seq, yg, o_att, gs, ga, g1, sc2, sh2, norm2_w, glu_w, up_w, w_out, peer_wq, k1, k2, tm=256):
    n, d = x2d.shape
    tm = min(tm, seq)
    assert seq % tm == 0
    bsz = n // seq
    row = lambda i: (i, 0)
    fixed = lambda i: (0, 0)
    per_b = lambda i: (i * tm // seq, 0, 0)
    full = lambda a: pl.BlockSpec(a.shape, fixed)
    weights = [w.astype(BF16) for w in (glu_w, up_w, w_out, peer_wq, k1, k2)]
    sc_shape = (PEER_HEADS, PEER_N_KEYS, n)
    return pl.pallas_call(
        _mid_kernel,
        grid=(n // tm,),
        in_specs=[
            pl.BlockSpec((tm, d), row),
            pl.BlockSpec((SSM_WIDTH, tm), lambda i: (0, i)),
            pl.BlockSpec((tm, ATTN_WIDTH), row),
            pl.BlockSpec((tm, d), row),
            pl.BlockSpec((tm, d), row),
            pl.BlockSpec((1, 1, d), per_b),
            pl.BlockSpec((1, 1, d), per_b),
            pl.BlockSpec((1, 1, d), per_b),
            pl.BlockSpec((1, d), fixed),
        ] + [full(w) for w in weights],
        out_specs=[
            pl.BlockSpec((tm, d), row),
            pl.BlockSpec((tm, d), row),
            pl.BlockSpec((PEER_HEADS, PEER_N_KEYS, tm), lambda i: (0, 0, i)),
            pl.BlockSpec((PEER_HEADS, PEER_N_KEYS, tm), lambda i: (0, 0, i)),
        ],
        out_shape=[jax.ShapeDtypeStruct((n, d), F32), jax.ShapeDtypeStruct((n, d), F32),
                   jax.ShapeDtypeStruct(sc_shape, F32), jax.ShapeDtypeStruct(sc_shape, F32)],
        compiler_params=_cparams("arbitrary"),
        name="mid_block",
    )(x2d, yg, o_att, gs, ga, g1.reshape(bsz, 1, d), sc2.reshape(bsz, 1, d), sh2.reshape(bsz, 1, d),
      norm2_w[None], *weights)


def _top16_rows(s):
    nrows = float(s.shape[0])
    rowid = lax.broadcasted_iota(jnp.int32, s.shape, 0).astype(F32)
    vals, ids = [], []
    for _ in range(PEER_TOPK):
        m = jnp.max(s, axis=0, keepdims=True)
        i = jnp.min(jnp.where(s == m, rowid, nrows), axis=0, keepdims=True)
        vals.append(m)
        ids.append(i)
        s = jnp.where(rowid == i, -jnp.inf, s)
    return jnp.concatenate(vals, axis=0), jnp.concatenate(ids, axis=0)


def _topk_kernel(s1_ref, s2_ref, g_ref, e_ref, *, width):
    def tile(ti, _):
        cols = pl.ds(pl.multiple_of(ti * width, width), width)
        gate, expert = _topk_tile(s1_ref[0, :, cols], s2_ref[0, :, cols])
        g_ref[0, :, cols] = gate
        e_ref[0, :, cols] = expert
        return 0

    lax.fori_loop(0, s1_ref.shape[2] // width, tile, 0)


def _topk_tile(s1, s2):
    v1, i1 = _top16_rows(s1)
    v2, i2 = _top16_rows(s2)
    half = PEER_TOPK // 2
    sub = lax.broadcasted_iota(jnp.int32, (half, v1.shape[1]), 0).astype(F32)
    cand, pay, flat = [], [], []
    for i in range(half):
        cand.append(v1[i:i + 1] + v2[:half])
        pay.append(i1[i:i + 1] * PEER_N_KEYS + i2[:half])
        flat.append(sub + i * PEER_TOPK)
    cand.append(v1[0:1] + v2[half:])
    pay.append(i1[0:1] * PEER_N_KEYS + i2[half:])
    flat.append(sub + half)
    cand.append(v1[half:] + v2[0:1])
    pay.append(i1[half:] * PEER_N_KEYS + i2[0:1])
    flat.append((sub + half) * PEER_TOPK)
    cand = jnp.concatenate(cand, axis=0)
    pay = jnp.concatenate(pay, axis=0)
    flat = jnp.concatenate(flat, axis=0)
    big = float(PEER_TOPK * PEER_TOPK)
    sc, ex = [], []
    for _ in range(PEER_TOPK):
        m = jnp.max(cand, axis=0, keepdims=True)
        f = jnp.min(jnp.where(cand == m, flat, big), axis=0, keepdims=True)
        sel = flat == f
        ex.append(jnp.max(jnp.where(sel, pay, -1.0), axis=0, keepdims=True))
        sc.append(m)
        cand = jnp.where(sel, -jnp.inf, cand)
    sc = jnp.concatenate(sc, axis=0)
    p = jnp.exp(sc - sc[0:1])
    return p / jnp.sum(p, axis=0, keepdims=True), jnp.concatenate(ex, axis=0).astype(jnp.int32)


def peer_topk(s1, s2, tn=2048, width=2 * LANES):
    heads, nkeys, n = s1.shape
    tn = min(tn, n)
    width = min(width, tn)
    assert n % tn == 0 and tn % width == 0
    blk_in = pl.BlockSpec((1, nkeys, tn), lambda h, i: (h, 0, i))
    blk_out = pl.BlockSpec((1, PEER_TOPK, tn), lambda h, i: (h, 0, i))
    return pl.pallas_call(
        functools.partial(_topk_kernel, width=width),
        grid=(heads, n // tn),
        in_specs=[blk_in, blk_in],
        out_specs=[blk_out, blk_out],
        out_shape=[jax.ShapeDtypeStruct((heads, PEER_TOPK, n), F32),
                   jax.ShapeDtypeStruct((heads, PEER_TOPK, n), jnp.int32)],
        compiler_params=_cparams("arbitrary", "arbitrary"),
        name="peer_topk",
    )(s1, s2)


PACK_ROWS = 2 * SUBLANES
EXPANDED = PEER_SLOTS * PACK_ROWS


def pack_expert_pairs(tab):
    n_exp, d = tab.shape
    half = n_exp // 2
    bits = lax.bitcast_convert_type(tab.astype(BF16), jnp.uint16).astype(jnp.uint32)
    words = (bits[:half] << 16) | bits[half:]
    return words.reshape(half * (d // LANES), LANES)


def _expand_matrix():
    k = np.arange(EXPANDED)
    return jnp.asarray((k[None, :] // PACK_ROWS) == np.arange(PEER_SLOTS)[:, None], BF16)


def _split2(x):
    hi = x.astype(BF16)
    lo = (x - hi.astype(F32)).astype(BF16)
    return hi, lo


def _gathered(tab_ref, row_ref, t):
    tok = row_ref.at[pl.ds(pl.multiple_of(t * PEER_SLOTS, PEER_SLOTS), PEER_SLOTS)]
    tiles = []
    for j in range(PEER_SLOTS):
        words = tab_ref[pl.ds(pl.multiple_of(tok[j], SUBLANES), SUBLANES), :]
        tiles.append(pltpu.bitcast(words, BF16))
    return jnp.concatenate(tiles, axis=0)


def _row_target(half_b):
    sub = lax.broadcasted_iota(jnp.int32, half_b.shape, 0).astype(F32)
    return half_b + 2.0 * sub


def _peer_u_kernel(row_ref, halfx_ref, h_ref, col_ref, tab_ref, r_ref, *, tb):
    kmod = (lax.broadcasted_iota(jnp.int32, (SUBLANES, EXPANDED), 1) % PACK_ROWS).astype(F32)
    tok_row = lax.broadcasted_iota(jnp.int32, (SUBLANES, EXPANDED), 0)
    nt = (((1,), (1,)), ((), ()))

    def group(gi, _):
        base = pl.multiple_of(gi * SUBLANES, SUBLANES)

        def token(i, acc):
            t = base + i
            hh = jnp.concatenate(_split2(h_ref[t]), axis=0)
            w = _gathered(tab_ref, row_ref, t)
            s = lax.dot_general(hh, w, nt, preferred_element_type=F32)
            both = s[:SUBLANES] + s[SUBLANES:]
            half_b = jnp.broadcast_to(halfx_ref[pl.ds(t, 1), :], (SUBLANES, EXPANDED))
            picked = jnp.where(kmod == _row_target(half_b), both, 0.0)
            red = jnp.broadcast_to(jnp.sum(picked, axis=0, keepdims=True), acc.shape)
            return jnp.where(tok_row == i, red, acc)

        acc = lax.fori_loop(0, SUBLANES, token, jnp.zeros((SUBLANES, EXPANDED), F32), unroll=4)
        tot = jnp.dot(jnp.concatenate(_split2(acc), axis=0), col_ref[...], preferred_element_type=F32)
        r_ref[pl.ds(base, SUBLANES), :] = tot[:SUBLANES] + tot[SUBLANES:]
        return 0

    lax.fori_loop(0, tb // SUBLANES, group, 0)


def _peer_v_kernel(row_ref, half_ref, r_ref, g_ref, x_ref, g2_ref, exp_ref, tab_ref, o_ref, x_scr, *, tb):
    g2 = g2_ref[0]
    kmod = (lax.broadcasted_iota(jnp.int32, (SUBLANES, EXPANDED), 1) % PACK_ROWS).astype(F32)

    def group(gi, _):
        base = pl.multiple_of(gi * SUBLANES, SUBLANES)
        coef = g_ref[pl.ds(base, SUBLANES), :] * jax.nn.gelu(r_ref[pl.ds(base, SUBLANES), :])
        c_hi, c_lo = _split2(coef)
        lhs = jnp.concatenate([c_hi, c_lo, half_ref[pl.ds(base, SUBLANES), :].astype(BF16)], axis=0)
        x_scr[...] = jnp.dot(lhs, exp_ref[...], preferred_element_type=F32)

        def token(i, _):
            t = base + i
            bc = lambda r: jnp.broadcast_to(x_scr[pl.ds(r, 1), :], (SUBLANES, EXPANDED))
            sel = kmod == _row_target(bc(2 * SUBLANES + i))
            l_hi = jnp.where(sel, bc(i), 0.0)
            l_lo = jnp.where(sel, bc(SUBLANES + i), 0.0)
            lmat = jnp.concatenate([l_hi, l_lo], axis=0).astype(BF16)
            out = jnp.dot(lmat, _gathered(tab_ref, row_ref, t), preferred_element_type=F32)
            o_ref[t] = x_ref[t] + g2 * (out[:SUBLANES] + out[SUBLANES:])
            return 0

        lax.fori_loop(0, SUBLANES, token, 0, unroll=8)
        return 0

    lax.fori_loop(0, tb // SUBLANES, group, 0)


def peer_u_dots(rows, half, h2, u_words, seq, tb=256):
    n = half.shape[0]
    tb = min(tb, seq)
    assert seq % tb == 0
    d_sub = D_MODEL // LANES
    half_x = jnp.repeat(half, PACK_ROWS, axis=1)
    return pl.pallas_call(
        functools.partial(_peer_u_kernel, tb=tb),
        grid=(n // tb,),
        in_specs=[
            pl.BlockSpec((tb * PEER_SLOTS,), lambda i: (i,), memory_space=pltpu.SMEM),
            pl.BlockSpec((tb, EXPANDED), lambda i: (i, 0)),
            pl.BlockSpec((tb, d_sub, LANES), lambda i: (i, 0, 0)),
            pl.BlockSpec((EXPANDED, PEER_SLOTS), lambda i: (0, 0)),
            pl.BlockSpec(memory_space=pltpu.VMEM),
        ],
        out_specs=pl.BlockSpec((tb, PEER_SLOTS), lambda i: (i, 0)),
        out_shape=jax.ShapeDtypeStruct((n, PEER_SLOTS), F32),
        compiler_params=_cparams("arbitrary"),
        name="peer_u_dots",
    )(rows, half_x, h2.reshape(n, d_sub, LANES), _expand_matrix().T, u_words)


def peer_v_mix(rows, half, r, gate, x1, g2, v_words, seq, tb=256):
    n = half.shape[0]
    tb = min(tb, seq)
    assert seq % tb == 0
    bsz = n // seq
    d_sub = D_MODEL // LANES
    tok = lambda i: (i, 0)
    tok3 = lambda i: (i, 0, 0)
    out = pl.pallas_call(
        functools.partial(_peer_v_kernel, tb=tb),
        grid=(n // tb,),
        in_specs=[
            pl.BlockSpec((tb * PEER_SLOTS,), lambda i: (i,), memory_space=pltpu.SMEM),
            pl.BlockSpec((tb, PEER_SLOTS), tok),
            pl.BlockSpec((tb, PEER_SLOTS), tok),
            pl.BlockSpec((tb, PEER_SLOTS), tok),
            pl.BlockSpec((tb, d_sub, LANES), tok3),
            pl.BlockSpec((1, d_sub, LANES), lambda i: (i * tb // seq, 0, 0)),
            pl.BlockSpec((PEER_SLOTS, EXPANDED), lambda i: (0, 0)),
            pl.BlockSpec(memory_space=pltpu.VMEM),
        ],
        out_specs=pl.BlockSpec((tb, d_sub, LANES), tok3),
        out_shape=jax.ShapeDtypeStruct((n, d_sub, LANES), F32),
        scratch_shapes=[pltpu.VMEM((3 * SUBLANES, EXPANDED), F32)],
        compiler_params=_cparams("arbitrary"),
        name="peer_v_mix",
    )(rows, half, r, gate, x1.reshape(n, d_sub, LANES), g2.reshape(bsz, d_sub, LANES), _expand_matrix(), v_words)
    return out.reshape(n, D_MODEL)


def peer_block(x1, h2, s1, s2, g2, u_tab, v_tab, seq):
    n = x1.shape[0]
    gate, expert = peer_topk(s1, s2)
    slots = lambda a: a.transpose(2, 0, 1).reshape(n, PEER_SLOTS)
    idx = slots(expert)
    rows = ((idx & (PEER_HALF_EXPERTS - 1)) * SUBLANES).reshape(-1)
    half = (1 - idx // PEER_HALF_EXPERTS).astype(F32)
    r = peer_u_dots(rows, half, h2, pack_expert_pairs(u_tab), seq)
    return peer_v_mix(rows, half, r, slots(gate), x1, g2, pack_expert_pairs(v_tab), seq)


def kernel(x, c, ada_w, ada_b, norm1_w, norm2_w, w_in, ssm_log_dt, ssm_a_re, ssm_a_im, ssm_b_re, ssm_b_im, ssm_c_re, ssm_c_im, ssm_d, ssm_glu_w, q_norm_w, k_norm_w, lambda_q1, lambda_k1, lambda_q2, lambda_k2, subln_w, attn_up_w, w_out, peer_wq, peer_k1, peer_k2, peer_u, peer_v):
    bsz, seq, d = x.shape
    n = bsz * seq
    depth = ada_w.shape[0]
    mod = ada_modulation(c, ada_w, ada_b)
    x2d = x.reshape(n, d)
    for l in range(depth):
        sh1, sc1, g1, sh2, sc2, g2 = jnp.split(mod[l], 6, axis=-1)
        u, q, k, v, gs, ga = in_projection(x2d, seq, norm1_w[l], sc1, sh1, w_in[l], q_norm_w[l], k_norm_w[l])
        mats = s5_matrices(ssm_log_dt[l], ssm_a_re[l], ssm_a_im[l], ssm_b_re[l], ssm_b_im[l],
                           ssm_c_re[l], ssm_c_im[l], ssm_d[l])
        yg = s5_mix(u, bsz, seq, *mats)
        lambda_init = 0.8 - 0.6 * math.exp(-0.3 * l)
        lam = (jnp.exp(jnp.sum(lambda_q1[l] * lambda_k1[l])) - jnp.exp(jnp.sum(lambda_q2[l] * lambda_k2[l]))
               + lambda_init)
        o_att = diff_attention(q, k, v, seq, lam, subln_w[l], lambda_init)
        x1, h2, s1, s2 = mid_block(x2d, seq, yg, o_att, gs, ga, g1, sc2, sh2, norm2_w[l], ssm_glu_w[l],
                                   attn_up_w[l], w_out[l], peer_wq[l], peer_k1[l], peer_k2[l])
        x2d = peer_block(x1, h2, s1, s2, g2, peer_u[l], peer_v[l], seq)
    return x2d.reshape(bsz, seq, d)
```

```python
import functools
import math

import jax
import jax.numpy as jnp
import numpy as np
from jax import lax
from jax.experimental import pallas as pl
from jax.experimental.pallas import tpu as pltpu

F32 = jnp.float32
BF16 = jnp.bfloat16

D_MODEL = 1024
SSM_GROUPS = 32
SSM_GROUP_CH = 16
SSM_WIDTH = SSM_GROUPS * SSM_GROUP_CH
SSM_STATE = 64
ATTN_HEADS = 4
ATTN_HEAD_DIM = 64
ATTN_V_DIM = 2 * ATTN_HEAD_DIM
QK_WIDTH = ATTN_HEADS * 2 * ATTN_HEAD_DIM
ATTN_WIDTH = ATTN_HEADS * ATTN_V_DIM
PEER_HEADS = 8
PEER_N_KEYS = 128
PEER_HALF = 128
PEER_TOPK = 16
PEER_SLOTS = PEER_HEADS * PEER_TOPK
PEER_HALF_EXPERTS = PEER_N_KEYS * PEER_N_KEYS // 2
RMS_EPS = 1e-6
NEG_BIG = -1e30
LOG2E = 1.4426950408889634

LANES = 128
SUBLANES = 8
S5_CHUNK = LANES
VMEM_LIMIT = 56 * 1024 * 1024


def _cparams(*sem):
    return pltpu.CompilerParams(dimension_semantics=sem, vmem_limit_bytes=VMEM_LIMIT)


def _ada_kernel(c_ref, w_ref, b_ref, o_ref):
    c = c_ref[...]
    ca = c * jax.nn.sigmoid(c)
    o_ref[0] = jnp.dot(ca, w_ref[0], preferred_element_type=F32) + b_ref[0]


def ada_modulation(c, ada_w, ada_b):
    depth, d, d6 = ada_w.shape
    bsz = c.shape[0]
    rows = -(-bsz // SUBLANES) * SUBLANES
    c_pad = jnp.zeros((rows, d), F32).at[:bsz].set(c)
    out = pl.pallas_call(
        _ada_kernel,
        grid=(depth, d6 // d),
        in_specs=[
            pl.BlockSpec((rows, d), lambda l, j: (0, 0)),
            pl.BlockSpec((1, d, d), lambda l, j: (l, 0, j)),
            pl.BlockSpec((1, 1, d), lambda l, j: (l, 0, j)),
        ],
        out_specs=pl.BlockSpec((1, rows, d), lambda l, j: (l, 0, j)),
        out_shape=jax.ShapeDtypeStruct((depth, rows, d6), F32),
        compiler_params=_cparams("arbitrary", "arbitrary"),
        name="ada_modulation",
    )(c_pad, ada_w, ada_b.reshape(depth, 1, d6))
    return out[:, :bsz]


def _group_norm64(z, gmat, w_row):
    ssq = jnp.dot((z * z).astype(BF16), gmat, preferred_element_type=F32)
    return z * lax.rsqrt(ssq * (1.0 / ATTN_HEAD_DIM) + RMS_EPS) * w_row


def _inproj_kernel(x_ref, nw_ref, sc_ref, sh_ref, w_ref, wut_ref, qw_ref, kw_ref, gmat_ref,
                   u_ref, q_ref, k_ref, v_ref, gs_ref, ga_ref):
    x = x_ref[...]
    ms = jnp.mean(x * x, axis=-1, keepdims=True)
    h = x * lax.rsqrt(ms + RMS_EPS) * nw_ref[...]
    h = h * (1.0 + sc_ref[0]) + sh_ref[0]
    hb = h.astype(BF16)
    o = 0
    u_ref[...] = lax.dot_general(wut_ref[...], hb, (((1,), (1,)), ((), ())),
                                 preferred_element_type=F32).astype(BF16)
    o += SSM_WIDTH
    q = jnp.dot(hb, w_ref[:, o:o + QK_WIDTH], preferred_element_type=F32)
    q_ref[...] = _group_norm64(q, gmat_ref[...], qw_ref[...]).astype(BF16)
    o += QK_WIDTH
    k = jnp.dot(hb, w_ref[:, o:o + QK_WIDTH], preferred_element_type=F32)
    k_ref[...] = _group_norm64(k, gmat_ref[...], kw_ref[...]).astype(BF16)
    o += QK_WIDTH
    v_ref[...] = jnp.dot(hb, w_ref[:, o:o + ATTN_WIDTH], preferred_element_type=F32).astype(BF16)
    o += ATTN_WIDTH
    gs = jnp.dot(hb, w_ref[:, o:o + D_MODEL], preferred_element_type=F32)
    gs_ref[...] = jax.nn.sigmoid(gs).astype(BF16)
    o += D_MODEL
    ga = jnp.dot(hb, w_ref[:, o:o + D_MODEL], preferred_element_type=F32)
    ga_ref[...] = jax.nn.sigmoid(ga).astype(BF16)


def in_projection(x2d, seq, norm_w, sc, sh, w_in, q_norm_w, k_norm_w, tm=512):
    n, d = x2d.shape
    tm = min(tm, seq)
    assert seq % tm == 0 and n % tm == 0
    bsz = n // seq
    width = w_in.shape[1]
    qw = jnp.tile(q_norm_w, QK_WIDTH // ATTN_HEAD_DIM)[None] * (ATTN_HEAD_DIM ** -0.5 * LOG2E)
    kw = jnp.tile(k_norm_w, QK_WIDTH // ATTN_HEAD_DIM)[None]
    grp = np.arange(QK_WIDTH) // ATTN_HEAD_DIM
    gmat = jnp.asarray(grp[:, None] == grp[None, :], BF16)
    row = lambda i: (i, 0)
    fixed = lambda i: (0, 0)
    per_b = lambda i: (i * tm // seq, 0, 0)
    out_w = (SSM_WIDTH, QK_WIDTH, QK_WIDTH, ATTN_WIDTH, D_MODEL, D_MODEL)
    return pl.pallas_call(
        _inproj_kernel,
        grid=(n // tm,),
        in_specs=[
            pl.BlockSpec((tm, d), row),
            pl.BlockSpec((1, d), fixed),
            pl.BlockSpec((1, 1, d), per_b),
            pl.BlockSpec((1, 1, d), per_b),
            pl.BlockSpec((d, width), fixed),
            pl.BlockSpec((SSM_WIDTH, d), fixed),
            pl.BlockSpec((1, QK_WIDTH), fixed),
            pl.BlockSpec((1, QK_WIDTH), fixed),
            pl.BlockSpec((QK_WIDTH, QK_WIDTH), fixed),
        ],
        out_specs=[pl.BlockSpec((SSM_WIDTH, tm), lambda i: (0, i))] + [pl.BlockSpec((tm, w), row) for w in out_w[1:]],
        out_shape=[jax.ShapeDtypeStruct((SSM_WIDTH, n), BF16)] + [jax.ShapeDtypeStruct((n, w), BF16) for w in out_w[1:]],
        compiler_params=_cparams("arbitrary"),
        name="in_projection",
    )(x2d, norm_w[None], sc.reshape(bsz, 1, d), sh.reshape(bsz, 1, d), w_in.astype(BF16),
      w_in[:, :SSM_WIDTH].T.astype(BF16), qw, kw, gmat)


def _attn_kernel(lam_ref, q_ref, k_ref, v_ref, sw_ref, o_ref, qq_scr, m_scr, acc_scr, *, blk, rows, out_scale):
    qi = pl.program_id(2)
    q = q_ref[...]
    lane = lax.broadcasted_iota(jnp.int32, q.shape, 1)
    zero = jnp.zeros_like(q)
    qq_scr[...] = jnp.concatenate([jnp.where(lane < ATTN_HEAD_DIM, q, zero),
                                   jnp.where(lane >= ATTN_HEAD_DIM, q, zero)], axis=0)
    ones = jnp.ones((blk, LANES), BF16)

    m_scr[...] = jnp.full(m_scr.shape, NEG_BIG, F32)
    acc_scr[...] = jnp.zeros(acc_scr.shape, F32)

    def step(ki, masked):
        off = pl.multiple_of(ki * blk, blk)
        kb = k_ref[pl.ds(off, blk), :]
        vb = jnp.concatenate([v_ref[pl.ds(off, blk), :], ones], axis=1)
        for r0 in range(0, 2 * blk, rows):
            rs = pl.ds(r0, rows)
            s = lax.dot_general(qq_scr[rs, :], kb, (((1,), (1,)), ((), ())), preferred_element_type=F32)
            if masked:
                row = lax.broadcasted_iota(jnp.int32, s.shape, 0) + (r0 % blk)
                col = lax.broadcasted_iota(jnp.int32, s.shape, 1)
                s = jnp.where(col <= row, s, NEG_BIG)
            m = m_scr[rs, :]
            m_new = jnp.maximum(m, jnp.max(s, axis=1, keepdims=True))
            alpha = jnp.exp2(m - m_new)
            p = jnp.exp2(s - m_new).astype(BF16)
            acc_scr[rs, :] = alpha * acc_scr[rs, :] + jnp.dot(p, vb, preferred_element_type=F32)
            m_scr[rs, :] = m_new

    @pl.loop(0, qi)
    def _(ki):
        step(ki, False)

    step(qi, True)
    acc = acc_scr[...]
    o_all = acc[:, :LANES] / acc[:, LANES:]
    o = o_all[:blk] - lam_ref[0] * o_all[blk:]
    ms = jnp.mean(o * o, axis=-1, keepdims=True)
    o_ref[...] = (o * lax.rsqrt(ms + RMS_EPS) * sw_ref[...] * out_scale).astype(BF16)


def diff_attention(q, k, v, seq, lam, subln_w, lambda_init, blk=1024, rows=128):
    n = q.shape[0]
    blk = min(blk, seq)
    rows = min(rows, blk)
    assert seq % blk == 0 and blk % rows == 0
    bsz = n // seq
    nq = seq // blk
    return pl.pallas_call(
        functools.partial(_attn_kernel, blk=blk, rows=rows, out_scale=1.0 - lambda_init),
        grid=(bsz, ATTN_HEADS, nq),
        in_specs=[
            pl.BlockSpec(memory_space=pltpu.SMEM),
            pl.BlockSpec((blk, ATTN_V_DIM), lambda b, h, i: (b * nq + i, h)),
            pl.BlockSpec((seq, ATTN_V_DIM), lambda b, h, i: (b, h)),
            pl.BlockSpec((seq, ATTN_V_DIM), lambda b, h, i: (b, h)),
            pl.BlockSpec((1, ATTN_V_DIM), lambda b, h, i: (0, 0)),
        ],
        out_specs=pl.BlockSpec((blk, ATTN_V_DIM), lambda b, h, i: (b * nq + i, h)),
        out_shape=jax.ShapeDtypeStruct((n, ATTN_WIDTH), BF16),
        scratch_shapes=[pltpu.VMEM((2 * blk, ATTN_V_DIM), BF16), pltpu.VMEM((2 * blk, 1), F32),
                        pltpu.VMEM((2 * blk, 2 * LANES), F32)],
        compiler_params=_cparams("arbitrary", "arbitrary", "arbitrary"),
        name="diff_attention",
    )(lam.reshape(1), q, k, v, subln_w[None])


def s5_matrices(log_dt, a_re, a_im, b_re, b_im, c_re, c_im, d_skip):
    t_len = S5_CHUNK
    g, p, h = b_re.shape
    dt = jnp.exp(log_dt)[:, None]
    mag = jnp.exp(a_re * dt)
    ab_re = mag * jnp.cos(a_im * dt)
    ab_im = mag * jnp.sin(a_im * dt)
    den = a_re * a_re + a_im * a_im
    nr = ab_re - 1.0
    f_re = (nr * a_re + ab_im * a_im) / den
    f_im = (ab_im * a_re - nr * a_im) / den
    bb_re = f_re[..., None] * b_re - f_im[..., None] * b_im
    bb_im = f_re[..., None] * b_im + f_im[..., None] * b_re
    tau = jnp.arange(t_len + 1, dtype=F32)[None, :, None]
    pmag = jnp.exp(a_re[:, None, :] * dt[:, None, :] * tau)
    pw_re = pmag * jnp.cos(a_im[:, None, :] * dt[:, None, :] * tau)
    pw_im = pmag * jnp.sin(a_im[:, None, :] * dt[:, None, :] * tau)
    ca_re = c_re[:, None] * pw_re[:, :, None, :] - c_im[:, None] * pw_im[:, :, None, :]
    ca_im = c_re[:, None] * pw_im[:, :, None, :] + c_im[:, None] * pw_re[:, :, None, :]
    kern = (jnp.einsum("gtop,gpi->gtoi", ca_re[:, :t_len], bb_re)
            - jnp.einsum("gtop,gpi->gtoi", ca_im[:, :t_len], bb_im))
    kern = kern.at[:, 0].add(jax.vmap(jnp.diag)(d_skip))
    lag = np.arange(t_len)[None, :] - np.arange(t_len)[:, None]
    onehot = jnp.asarray(lag[None] == np.arange(t_len)[:, None, None], BF16)
    toep = jnp.einsum("gloi,lst->gisot", kern.astype(BF16), onehot, preferred_element_type=BF16)
    toep = toep.reshape(g, h * t_len, h * t_len)
    rp_re = pw_re[:, t_len - 1::-1][:, :t_len]
    rp_im = pw_im[:, t_len - 1::-1][:, :t_len]
    w_re = rp_re[..., None] * bb_re[:, None] - rp_im[..., None] * bb_im[:, None]
    w_im = rp_re[..., None] * bb_im[:, None] + rp_im[..., None] * bb_re[:, None]
    w_st = jnp.concatenate([w_re, w_im], axis=2).transpose(0, 3, 1, 2).reshape(g, h * t_len, 2 * p)
    v_st = jnp.concatenate([ca_re[:, 1:], -ca_im[:, 1:]], axis=3)
    v_st = v_st.transpose(0, 3, 2, 1).reshape(g, 2 * p, h * t_len)
    at_re, at_im = pw_re[:, t_len], pw_im[:, t_len]
    coef = jnp.stack([jnp.concatenate([at_re, at_re], -1), jnp.concatenate([-at_im, at_im], -1),
                      jnp.concatenate([at_im, -at_im], -1)], axis=1)
    return toep.astype(BF16), w_st.astype(BF16), v_st.astype(BF16), coef


def _s5_kernel(u_ref, toep_ref, w_ref, v_ref, coef_ref, y_ref, s_scr, sw_scr, xp_scr, *, bsz, nchunk):
    hch = u_ref.shape[1]
    xin = jnp.concatenate([u_ref[0, i] for i in range(hch)], axis=1)
    y = jnp.dot(xin, toep_ref[0], preferred_element_type=F32)
    s = jnp.dot(xin, w_ref[0], preferred_element_type=F32)
    half = s.shape[1] // 2
    s_scr[...] = s
    sw_scr[...] = pltpu.roll(s, half, 1)
    coef = coef_ref[0]
    a_cat, b_cat, b_swp = coef[0:1], coef[1:2], coef[2:3]

    def body(c8, carry):
        new = []
        for b in range(bsz):
            x, xs = carry[2 * b], carry[2 * b + 1]
            base = pl.multiple_of(b * nchunk + c8 * SUBLANES, SUBLANES)
            st = s_scr[pl.ds(base, SUBLANES), :]
            sw = sw_scr[pl.ds(base, SUBLANES), :]
            prev = []
            for i in range(SUBLANES):
                prev.append(x)
                x, xs = (a_cat * x + b_cat * xs + st[i:i + 1], a_cat * xs + b_swp * x + sw[i:i + 1])
            xp_scr[pl.ds(base, SUBLANES), :] = jnp.concatenate(prev, axis=0)
            new += [x, xs]
        return tuple(new)

    zero = jnp.zeros((1, s.shape[1]), F32)
    lax.fori_loop(0, nchunk // SUBLANES, body, (zero,) * (2 * bsz))
    xp = xp_scr[...]
    xp_hi = xp.astype(BF16)
    xp_lo = (xp - xp_hi.astype(F32)).astype(BF16)
    y = y + jnp.dot(xp_hi, v_ref[0], preferred_element_type=F32) + jnp.dot(xp_lo, v_ref[0], preferred_element_type=F32)
    y = jax.nn.gelu(y)
    t_len = y.shape[1] // hch
    for i in range(hch):
        y_ref[0, i] = y[:, i * t_len:(i + 1) * t_len].astype(BF16)


def s5_mix(u_t, bsz, seq, toep, w_st, v_st, coef):
    t_len = S5_CHUNK
    n = bsz * seq
    assert seq % (t_len * SUBLANES) == 0
    nchunk = seq // t_len
    rows = n // t_len
    g, hch = SSM_GROUPS, SSM_GROUP_CH
    u4 = u_t.reshape(g, hch, rows, t_len)
    kt = hch * t_len
    p2 = 2 * SSM_STATE
    grp = lambda i: (i, 0, 0)
    y4 = pl.pallas_call(
        functools.partial(_s5_kernel, bsz=bsz, nchunk=nchunk),
        grid=(g,),
        in_specs=[
            pl.BlockSpec((1, hch, rows, t_len), lambda i: (i, 0, 0, 0)),
            pl.BlockSpec((1, kt, kt), grp),
            pl.BlockSpec((1, kt, p2), grp),
            pl.BlockSpec((1, p2, kt), grp),
            pl.BlockSpec((1, 3, p2), grp),
        ],
        out_specs=pl.BlockSpec((1, hch, rows, t_len), lambda i: (i, 0, 0, 0)),
        out_shape=jax.ShapeDtypeStruct((g, hch, rows, t_len), BF16),
        scratch_shapes=[pltpu.VMEM((rows, p2), F32)] * 3,
        compiler_params=_cparams("arbitrary"),
        name="s5_mix",
    )(u4, toep, w_st, v_st, coef)
    return y4.reshape(g * hch, n)


def _mid_kernel(x_ref, yg_ref, oa_ref, gs_ref, ga_ref, g1_ref, sc2_ref, sh2_ref, nw2_ref,
                glu_ref, up_ref, wo_ref, wq_ref, k1_ref, k2_ref,
                x1_ref, h2_ref, s1_ref, s2_ref):
    vg = lax.dot_general(yg_ref[...], glu_ref[...], (((0,), (0,)), ((), ())), preferred_element_type=F32)
    y_ssm = vg[:, :D_MODEL] * jax.nn.sigmoid(vg[:, D_MODEL:])
    y_att = jnp.dot(oa_ref[...], up_ref[...], preferred_element_type=F32)
    merged = gs_ref[...].astype(F32) * y_ssm + ga_ref[...].astype(F32) * y_att
    x1 = x_ref[...] + g1_ref[0] * jnp.dot(merged.astype(BF16), wo_ref[...], preferred_element_type=F32)
    x1_ref[...] = x1
    ms = jnp.mean(x1 * x1, axis=-1, keepdims=True)
    h2 = x1 * lax.rsqrt(ms + RMS_EPS) * nw2_ref[...]
    h2 = h2 * (1.0 + sc2_ref[0]) + sh2_ref[0]
    h2_ref[...] = h2
    qp = jnp.dot(h2.astype(BF16), wq_ref[...], preferred_element_type=F32).astype(BF16)
    nt = (((1,), (1,)), ((), ()))
    for hh in range(PEER_HEADS):
        o = hh * 2 * PEER_HALF
        s1_ref[hh] = lax.dot_general(k1_ref[...], qp[:, o:o + PEER_HALF], nt, preferred_element_type=F32)
        s2_ref[hh] = lax.dot_general(k2_ref[...], qp[:, o + PEER_HALF:o + 2 * PEER_HALF], nt,
                                     preferred_element_type=F32)


def mid_block(x2d, seq, yg, o_att, gs, ga, g1, sc2, sh2, norm2_w, glu_w, up_w, w_out, peer_wq, k1, k2, tm=256):
    n, d = x2d.shape
    tm = min(tm, seq)
    assert seq % tm == 0
    bsz = n // seq
    row = lambda i: (i, 0)
    fixed = lambda i: (0, 0)
    per_b = lambda i: (i * tm // seq, 0, 0)
    full = lambda a: pl.BlockSpec(a.shape, fixed)
    weights = [w.astype(BF16) for w in (glu_w, up_w, w_out, peer_wq, k1, k2)]
    sc_shape = (PEER_HEADS, PEER_N_KEYS, n)
    return pl.pallas_call(
        _mid_kernel,
        grid=(n // tm,),
        in_specs=[
            pl.BlockSpec((tm, d), row),
            pl.BlockSpec((SSM_WIDTH, tm), lambda i: (0, i)),
            pl.BlockSpec((tm, ATTN_WIDTH), row),
            pl.BlockSpec((tm, d), row),
            pl.BlockSpec((tm, d), row),
            pl.BlockSpec((1, 1, d), per_b),
            pl.BlockSpec((1, 1, d), per_b),
            pl.BlockSpec((1, 1, d), per_b),
            pl.BlockSpec((1, d), fixed),
        ] + [full(w) for w in weights],
        out_specs=[
            pl.BlockSpec((tm, d), row),
            pl.BlockSpec((tm, d), row),
            pl.BlockSpec((PEER_HEADS, PEER_N_KEYS, tm), lambda i: (0, 0, i)),
            pl.BlockSpec((PEER_HEADS, PEER_N_KEYS, tm), lambda i: (0, 0, i)),
        ],
        out_shape=[jax.ShapeDtypeStruct((n, d), F32), jax.ShapeDtypeStruct((n, d), F32),
                   jax.ShapeDtypeStruct(sc_shape, F32), jax.ShapeDtypeStruct(sc_shape, F32)],
        compiler_params=_cparams("arbitrary"),
        name="mid_block",
    )(x2d, yg, o_att, gs, ga, g1.reshape(bsz, 1, d), sc2.reshape(bsz, 1, d), sh2.reshape(bsz, 1, d),
      norm2_w[None], *weights)


def _top16_rows(s):
    nrows = float(s.shape[0])
    rowid = lax.broadcasted_iota(jnp.int32, s.shape, 0).astype(F32)
    vals, ids = [], []
    for _ in range(PEER_TOPK):
        m = jnp.max(s, axis=0, keepdims=True)
        i = jnp.min(jnp.where(s == m, rowid, nrows), axis=0, keepdims=True)
        vals.append(m)
        ids.append(i)
        s = jnp.where(rowid == i, -jnp.inf, s)
    return jnp.concatenate(vals, axis=0), jnp.concatenate(ids, axis=0)


def _topk_kernel(s1_ref, s2_ref, g_ref, e_ref, *, width):
    def tile(ti, _):
        cols = pl.ds(pl.multiple_of(ti * width, width), width)
        gate, expert = _topk_tile(s1_ref[0, :, cols], s2_ref[0, :, cols])
        g_ref[0, :, cols] = gate
        e_ref[0, :, cols] = expert
        return 0

    lax.fori_loop(0, s1_ref.shape[2] // width, tile, 0)


def _topk_tile(s1, s2):
    v1, i1 = _top16_rows(s1)
    v2, i2 = _top16_rows(s2)
    half = PEER_TOPK // 2
    sub = lax.broadcasted_iota(jnp.int32, (half, v1.shape[1]), 0).astype(F32)
    cand, pay, flat = [], [], []
    for i in range(half):
        cand.append(v1[i:i + 1] + v2[:half])
        pay.append(i1[i:i + 1] * PEER_N_KEYS + i2[:half])
        flat.append(sub + i * PEER_TOPK)
    cand.append(v1[0:1] + v2[half:])
    pay.append(i1[0:1] * PEER_N_KEYS + i2[half:])
    flat.append(sub + half)
    cand.append(v1[half:] + v2[0:1])
    pay.append(i1[half:] * PEER_N_KEYS + i2[0:1])
    flat.append((sub + half) * PEER_TOPK)
    cand = jnp.concatenate(cand, axis=0)
    pay = jnp.concatenate(pay, axis=0)
    flat = jnp.concatenate(flat, axis=0)
    big = float(PEER_TOPK * PEER_TOPK)
    sc, ex = [], []
    for _ in range(PEER_TOPK):
        m = jnp.max(cand, axis=0, keepdims=True)
        f = jnp.min(jnp.where(cand == m, flat, big), axis=0, keepdims=True)
        sel = flat == f
        ex.append(jnp.max(jnp.where(sel, pay, -1.0), axis=0, keepdims=True))
        sc.append(m)
        cand = jnp.where(sel, -jnp.inf, cand)
    sc = jnp.concatenate(sc, axis=0)
    p = jnp.exp(sc - sc[0:1])
    return p / jnp.sum(p, axis=0, keepdims=True), jnp.concatenate(ex, axis=0).astype(jnp.int32)


def peer_topk(s1, s2, tn=2048, width=2 * LANES):
    heads, nkeys, n = s1.shape
    tn = min(tn, n)
    width = min(width, tn)
    assert n % tn == 0 and tn % width == 0
    blk_in = pl.BlockSpec((1, nkeys, tn), lambda h, i: (h, 0, i))
    blk_out = pl.BlockSpec((1, PEER_TOPK, tn), lambda h, i: (h, 0, i))
    return pl.pallas_call(
        functools.partial(_topk_kernel, width=width),
        grid=(heads, n // tn),
        in_specs=[blk_in, blk_in],
        out_specs=[blk_out, blk_out],
        out_shape=[jax.ShapeDtypeStruct((heads, PEER_TOPK, n), F32),
                   jax.ShapeDtypeStruct((heads, PEER_TOPK, n), jnp.int32)],
        compiler_params=_cparams("arbitrary", "arbitrary"),
        name="peer_topk",
    )(s1, s2)


PACK_ROWS = 2 * SUBLANES
EXPANDED = PEER_SLOTS * PACK_ROWS


def pack_expert_pairs(tab):
    n_exp, d = tab.shape
    half = n_exp // 2
    bits = lax.bitcast_convert_type(tab.astype(BF16), jnp.uint16).astype(jnp.uint32)
    words = (bits[:half] << 16) | bits[half:]
    return words.reshape(half * (d // LANES), LANES)


def _expand_matrix():
    k = np.arange(EXPANDED)
    return jnp.asarray((k[None, :] // PACK_ROWS) == np.arange(PEER_SLOTS)[:, None], BF16)


def _split2(x):
    hi = x.astype(BF16)
    lo = (x - hi.astype(F32)).astype(BF16)
    return hi, lo


def _gathered(tab_ref, row_ref, t):
    tok = row_ref.at[pl.ds(pl.multiple_of(t * PEER_SLOTS, PEER_SLOTS), PEER_SLOTS)]
    tiles = []
    for j in range(PEER_SLOTS):
        words = tab_ref[pl.ds(pl.multiple_of(tok[j], SUBLANES), SUBLANES), :]
        tiles.append(pltpu.bitcast(words, BF16))
    return jnp.concatenate(tiles, axis=0)


def _row_target(half_b):
    sub = lax.broadcasted_iota(jnp.int32, half_b.shape, 0).astype(F32)
    return half_b + 2.0 * sub


def _peer_u_kernel(row_ref, halfx_ref, h_ref, col_ref, tab_ref, r_ref, *, tb):
    kmod = (lax.broadcasted_iota(jnp.int32, (SUBLANES, EXPANDED), 1) % PACK_ROWS).astype(F32)
    tok_row = lax.broadcasted_iota(jnp.int32, (SUBLANES, EXPANDED), 0)
    nt = (((1,), (1,)), ((), ()))

    def group(gi, _):
        base = pl.multiple_of(gi * SUBLANES, SUBLANES)

        def token(i, acc):
            t = base + i
            hh = jnp.concatenate(_split2(h_ref[t]), axis=0)
            w = _gathered(tab_ref, row_ref, t)
            s = lax.dot_general(hh, w, nt, preferred_element_type=F32)
            both = s[:SUBLANES] + s[SUBLANES:]
            half_b = jnp.broadcast_to(halfx_ref[pl.ds(t, 1), :], (SUBLANES, EXPANDED))
            picked = jnp.where(kmod == _row_target(half_b), both, 0.0)
            red = jnp.broadcast_to(jnp.sum(picked, axis=0, keepdims=True), acc.shape)
            return jnp.where(tok_row == i, red, acc)

        acc = lax.fori_loop(0, SUBLANES, token, jnp.zeros((SUBLANES, EXPANDED), F32), unroll=8)
        tot = jnp.dot(jnp.concatenate(_split2(acc), axis=0), col_ref[...], preferred_element_type=F32)
        r_ref[pl.ds(base, SUBLANES), :] = tot[:SUBLANES] + tot[SUBLANES:]
        return 0

    lax.fori_loop(0, tb // SUBLANES, group, 0)


def _peer_v_kernel(row_ref, half_ref, r_ref, g_ref, x_ref, g2_ref, exp_ref, tab_ref, o_ref, x_scr, *, tb):
    g2 = g2_ref[0]
    kmod = (lax.broadcasted_iota(jnp.int32, (SUBLANES, EXPANDED), 1) % PACK_ROWS).astype(F32)

    def group(gi, _):
        base = pl.multiple_of(gi * SUBLANES, SUBLANES)
        coef = g_ref[pl.ds(base, SUBLANES), :] * jax.nn.gelu(r_ref[pl.ds(base, SUBLANES), :])
        c_hi, c_lo = _split2(coef)
        lhs = jnp.concatenate([c_hi, c_lo, half_ref[pl.ds(base, SUBLANES), :].astype(BF16)], axis=0)
        x_scr[...] = jnp.dot(lhs, exp_ref[...], preferred_element_type=F32)

        def token(i, _):
            t = base + i
            bc = lambda r: jnp.broadcast_to(x_scr[pl.ds(r, 1), :], (SUBLANES, EXPANDED))
            sel = kmod == _row_target(bc(2 * SUBLANES + i))
            l_hi = jnp.where(sel, bc(i), 0.0)
            l_lo = jnp.where(sel, bc(SUBLANES + i), 0.0)
            lmat = jnp.concatenate([l_hi, l_lo], axis=0).astype(BF16)
            out = jnp.dot(lmat, _gathered(tab_ref, row_ref, t), preferred_element_type=F32)
            o_ref[t] = x_ref[t] + g2 * (out[:SUBLANES] + out[SUBLANES:])
            return 0

        lax.fori_loop(0, SUBLANES, token, 0, unroll=8)
        return 0

    lax.fori_loop(0, tb // SUBLANES, group, 0)


def peer_u_dots(rows, half, h2, u_words, seq, tb=256):
    n = half.shape[0]
    tb = min(tb, seq)
    assert seq % tb == 0
    d_sub = D_MODEL // LANES
    half_x = jnp.repeat(half, PACK_ROWS, axis=1)
    return pl.pallas_call(
        functools.partial(_peer_u_kernel, tb=tb),
        grid=(n // tb,),
        in_specs=[
            pl.BlockSpec((tb * PEER_SLOTS,), lambda i: (i,), memory_space=pltpu.SMEM),
            pl.BlockSpec((tb, EXPANDED), lambda i: (i, 0)),
            pl.BlockSpec((tb, d_sub, LANES), lambda i: (i, 0, 0)),
            pl.BlockSpec((EXPANDED, PEER_SLOTS), lambda i: (0, 0)),
            pl.BlockSpec(memory_space=pltpu.VMEM),
        ],
        out_specs=pl.BlockSpec((tb, PEER_SLOTS), lambda i: (i, 0)),
        out_shape=jax.ShapeDtypeStruct((n, PEER_SLOTS), F32),
        compiler_params=_cparams("arbitrary"),
        name="peer_u_dots",
    )(rows, half_x, h2.reshape(n, d_sub, LANES), _expand_matrix().T, u_words)


def peer_v_mix(rows, half, r, gate, x1, g2, v_words, seq, tb=256):
    n = half.shape[0]
    tb = min(tb, seq)
    assert seq % tb == 0
    bsz = n // seq
    d_sub = D_MODEL // LANES
    tok = lambda i: (i, 0)
    tok3 = lambda i: (i, 0, 0)
    out = pl.pallas_call(
        functools.partial(_peer_v_kernel, tb=tb),
        grid=(n // tb,),
        in_specs=[
            pl.BlockSpec((tb * PEER_SLOTS,), lambda i: (i,), memory_space=pltpu.SMEM),
            pl.BlockSpec((tb, PEER_SLOTS), tok),
            pl.BlockSpec((tb, PEER_SLOTS), tok),
            pl.BlockSpec((tb, PEER_SLOTS), tok),
            pl.BlockSpec((tb, d_sub, LANES), tok3),
            pl.BlockSpec((1, d_sub, LANES), lambda i: (i * tb // seq, 0, 0)),
            pl.BlockSpec((PEER_SLOTS, EXPANDED), lambda i: (0, 0)),
            pl.BlockSpec(memory_space=pltpu.VMEM),
        ],
        out_specs=pl.BlockSpec((tb, d_sub, LANES), tok3),
        out_shape=jax.ShapeDtypeStruct((n, d_sub, LANES), F32),
        scratch_shapes=[pltpu.VMEM((3 * SUBLANES, EXPANDED), F32)],
        compiler_params=_cparams("arbitrary"),
        name="peer_v_mix",
    )(rows, half, r, gate, x1.reshape(n, d_sub, LANES), g2.reshape(bsz, d_sub, LANES), _expand_matrix(), v_words)
    return out.reshape(n, D_MODEL)


def peer_block(x1, h2, s1, s2, g2, u_tab, v_tab, seq):
    n = x1.shape[0]
    gate, expert = peer_topk(s1, s2)
    slots = lambda a: a.transpose(2, 0, 1).reshape(n, PEER_SLOTS)
    idx = slots(expert)
    rows = ((idx & (PEER_HALF_EXPERTS - 1)) * SUBLANES).reshape(-1)
    half = (1 - idx // PEER_HALF_EXPERTS).astype(F32)
    r = peer_u_dots(rows, half, h2, pack_expert_pairs(u_tab), seq)
    return peer_v_mix(rows, half, r, slots(gate), x1, g2, pack_expert_pairs(v_tab), seq)


def kernel(x, c, ada_w, ada_b, norm1_w, norm2_w, w_in, ssm_log_dt, ssm_a_re, ssm_a_im, ssm_b_re, ssm_b_im, ssm_c_re, ssm_c_im, ssm_d, ssm_glu_w, q_norm_w, k_norm_w, lambda_q1, lambda_k1, lambda_q2, lambda_k2, subln_w, attn_up_w, w_out, peer_wq, peer_k1, peer_k2, peer_u, peer_v):
    bsz, seq, d = x.shape
    n = bsz * seq
    depth = ada_w.shape[0]
    mod = ada_modulation(c, ada_w, ada_b)
    x2d = x.reshape(n, d)
    for l in range(depth):
        sh1, sc1, g1, sh2, sc2, g2 = jnp.split(mod[l], 6, axis=-1)
        u, q, k, v, gs, ga = in_projection(x2d, seq, norm1_w[l], sc1, sh1, w_in[l], q_norm_w[l], k_norm_w[l])
        mats = s5_matrices(ssm_log_dt[l], ssm_a_re[l], ssm_a_im[l], ssm_b_re[l], ssm_b_im[l],
                           ssm_c_re[l], ssm_c_im[l], ssm_d[l])
        yg = s5_mix(u, bsz, seq, *mats)
        lambda_init = 0.8 - 0.6 * math.exp(-0.3 * l)
        lam = (jnp.exp(jnp.sum(lambda_q1[l] * lambda_k1[l])) - jnp.exp(jnp.sum(lambda_q2[l] * lambda_k2[l]))
               + lambda_init)
        o_att = diff_attention(q, k, v, seq, lam, subln_w[l], lambda_init)
        x1, h2, s1, s2 = mid_block(x2d, seq, yg, o_att, gs, ga, g1, sc2, sh2, norm2_w[l], ssm_glu_w[l],
                                   attn_up_w[l], w_out[l], peer_wq[l], peer_k1[l], peer_k2[l])
        x2d = peer_block(x1, h2, s1, s2, g2, peer_u[l], peer_v[l], seq)
    return x2d.reshape(bsz, seq, d)
```
